```python
import math
import jax, jax.numpy as jnp
from jax import lax
import numpy as np

D_MODEL = 1024
BATCH = 16
SEQ = 2048
DEPTH = 4

CTX_LEN = 256
GRID_W = 64
CONV_CH = 512
CONV_WIDTH = 31
MLA_HEADS = 8
QK_NOPE = 64
QK_ROPE = 32
V_HEAD = 64
Q_LORA = 768
KV_LORA = 256
QK_HEAD = QK_NOPE + QK_ROPE
REC_WIDTH = 512
REC_BLOCKS = 8
REC_BW = REC_WIDTH // REC_BLOCKS
REC_CONV = 4
LRU_C = 8.0
N_BRANCH = 3
D_FF = 3584
N_EXPERTS = 8
TOP_K = 2
N_DENSE = DEPTH - DEPTH // 2
N_MOE = DEPTH // 2
ROPE_BASE = 10000.0
NORM_EPS = 1e-6
Q_BLOCK = 128
COL_CONV = 2 * CONV_CH
COL_Q = Q_LORA
COL_KV = KV_LORA + QK_ROPE
COL_REC = 2 * REC_WIDTH
COL_GATE = N_BRANCH * D_MODEL
IN_COLS = COL_CONV + COL_Q + COL_KV + COL_REC + COL_GATE

kernel_name = "hybrid_conv_mla_rglru_moe_dit"


def rmsnorm(x, g):
    xf = x.astype(jnp.float32)
    y = xf * lax.rsqrt(jnp.mean(xf * xf, axis=-1, keepdims=True) + NORM_EPS)
    return (y * g.astype(jnp.float32)).astype(x.dtype)


def layernorm(x, g, b):
    xf = x.astype(jnp.float32)
    mu = jnp.mean(xf, axis=-1, keepdims=True)
    var = jnp.mean(jnp.square(xf - mu), axis=-1, keepdims=True)
    y = (xf - mu) * lax.rsqrt(var + NORM_EPS)
    return (y * g.astype(jnp.float32) + b.astype(jnp.float32)).astype(x.dtype)


def modulate(h, shift, scale):
    return h * (1 + scale) + shift


def depthwise_conv(x, w, b, pad):
    y = lax.conv_general_dilated(x, w[:, None, :].astype(x.dtype), window_strides=(1,), padding=(pad,),
                                 dimension_numbers=("NWC", "WIO", "NWC"), feature_group_count=x.shape[-1])
    return y + b


def axial_rope_tables(seq):
    rows = seq // GRID_W
    row = jnp.repeat(jnp.arange(rows, dtype=jnp.int32), GRID_W)
    col = jnp.tile(jnp.arange(GRID_W, dtype=jnp.int32), rows)
    half = QK_ROPE // 2
    freqs = ROPE_BASE ** (-jnp.arange(0, half, 2, dtype=jnp.float32) / half)
    ang = jnp.stack([row[:, None].astype(jnp.float32) * freqs,
                     col[:, None].astype(jnp.float32) * freqs], axis=1)
    return jnp.cos(ang), jnp.sin(ang)


def apply_axial_rope(t, cos, sin):
    ts = t.astype(jnp.float32).reshape(t.shape[:-1] + (2, 2, QK_ROPE // 4))
    t1, t2 = ts[..., 0, :], ts[..., 1, :]
    cs, sn = cos[None, :, None], sin[None, :, None]
    out = jnp.stack([t1 * cs - t2 * sn, t2 * cs + t1 * sn], axis=-2)
    return out.reshape(t.shape).astype(t.dtype)


def rope_heads(t, cos, sin):
    return jnp.concatenate([t[..., :QK_NOPE], apply_axial_rope(t[..., QK_NOPE:], cos, sin)], axis=-1)


def split_cols(z):
    o1 = COL_CONV
    o2 = o1 + COL_Q
    o3 = o2 + COL_KV
    o4 = o3 + COL_REC
    return z[..., :o1], z[..., o1:o2], z[..., o2:o3], z[..., o3:o4], z[..., o4:]


def conv_branch(z, p):
    u = z[..., :CONV_CH] * jax.nn.sigmoid(z[..., CONV_CH:])
    u = depthwise_conv(u, p["conv_w"], p["conv_b"], (CONV_WIDTH // 2, CONV_WIDTH // 2))
    u = layernorm(u, p["conv_ln_g"], p["conv_ln_b"])
    return jax.nn.silu(u) @ p["w_o_conv"]


def mla_q(zq, p):
    b, l = zq.shape[:2]
    q = (rmsnorm(zq, p["g_q_a"]) @ p["w_q_b"]).reshape(b, l, MLA_HEADS, QK_HEAD)
    return rmsnorm(q, p["g_qn"])


def mla_kv(zkv, p):
    b, l = zkv.shape[:2]
    ckv = rmsnorm(zkv[..., :KV_LORA], p["g_kv_a"])
    k_rope = zkv[..., KV_LORA:]
    kv = (ckv @ p["w_kv_b"]).reshape(b, l, MLA_HEADS, QK_NOPE + V_HEAD)
    k = jnp.concatenate([kv[..., :QK_NOPE],
                         jnp.broadcast_to(k_rope[:, :, None, :], (b, l, MLA_HEADS, QK_ROPE))], axis=-1)
    return rmsnorm(k, p["g_kn"]), kv[..., QK_NOPE:]


def attend(q, k, v):
    s = jnp.einsum("bqhd,bkhd->bhqk", q, k).astype(jnp.float32) * (1.0 / math.sqrt(QK_HEAD))
    pr = jax.nn.softmax(s, axis=-1).astype(v.dtype)
    return jnp.einsum("bhqk,bkhd->bqhd", pr, v)


def block_attention(q, k, v):
    b, s, h, dq = q.shape
    nb = s // Q_BLOCK
    qb = q.reshape(b, nb, Q_BLOCK, h, dq).transpose(1, 0, 2, 3, 4)
    out = lax.map(lambda qq: attend(qq, k, v), qb)
    return out.transpose(1, 0, 2, 3, 4).reshape(b, s, h, V_HEAD)


def lru_scan(a, bx, h0, reverse):
    def comb(e1, e2):
        a1, b1 = e1
        a2, b2 = e2
        return a1 * a2, a2 * b1 + b2
    a_cum, b_cum = lax.associative_scan(comb, (a, bx), reverse=reverse, axis=1)
    return b_cum + a_cum * h0[:, None, :]


def rglru_dir(xc, w_a, b_a, w_i, b_i, lam, h0, reverse):
    b, l, w = xc.shape
    xh = xc.reshape(b, l, REC_BLOCKS, REC_BW)
    r = jax.nn.sigmoid(jnp.einsum("blgi,gij->blgj", xh, w_a).reshape(b, l, w) + b_a).astype(jnp.float32)
    i = jax.nn.sigmoid(jnp.einsum("blgi,gij->blgj", xh, w_i).reshape(b, l, w) + b_i).astype(jnp.float32)
    log_a = -LRU_C * r * jax.nn.softplus(-lam.astype(jnp.float32))
    a = jnp.exp(log_a)
    bx = jnp.sqrt(-jnp.expm1(2.0 * log_a)) * (i * xc.astype(jnp.float32))
    return lru_scan(a, bx, h0, reverse)


def rec_branch(zl, zc, p, ctx_out):
    b = zl.shape[0]
    hl_dirs, hc_dirs = [], []
    for d, reverse in ((0, False), (1, True)):
        pad = (0, REC_CONV - 1) if reverse else (REC_CONV - 1, 0)
        xc_c = depthwise_conv(zc[..., :REC_WIDTH], p["rec_conv_w"][d], p["rec_conv_b"][d], pad)
        xc_l = depthwise_conv(zl[..., :REC_WIDTH], p["rec_conv_w"][d], p["rec_conv_b"][d], pad)
        args = (p["w_ra"][d], p["b_ra"][d], p["w_ri"][d], p["b_ri"][d], p["lru_lambda"][d])
        hc = rglru_dir(xc_c, *args, jnp.zeros((b, REC_WIDTH), jnp.float32), reverse)
        h_fin = hc[:, 0] if reverse else hc[:, -1]
        hl_dirs.append(rglru_dir(xc_l, *args, h_fin, reverse))
        hc_dirs.append(hc)
    yl = ((hl_dirs[0] + hl_dirs[1]).astype(zl.dtype) * jax.nn.gelu(zl[..., REC_WIDTH:])) @ p["w_o_rec"]
    yc = None
    if ctx_out:
        yc = ((hc_dirs[0] + hc_dirs[1]).astype(zc.dtype) * jax.nn.gelu(zc[..., REC_WIDTH:])) @ p["w_o_rec"]
    return yl, yc


def gated_merge(zg, ya, yb, yc, p):
    g = jax.nn.sigmoid(zg + p["b_gate"]).reshape(zg.shape[:-1] + (N_BRANCH, D_MODEL))
    m = g[..., 0, :] * ya + g[..., 1, :] * yb + g[..., 2, :] * yc
    return m @ p["w_out"]


def mixer(hl, hc, p, cos, sin, ctx_out):
    b, s, _ = hl.shape
    cl, ql, kvl, rl, gl = split_cols(hl @ p["w_in"])
    cc, qc, kvc, rc, gc = split_cols(hc @ p["w_in"])
    ya_l = conv_branch(cl, p)
    kc, vc = mla_kv(kvc, p)
    kl, vl = mla_kv(kvl, p)
    kl = rope_heads(kl, cos, sin)
    q_lat = rope_heads(mla_q(ql, p), cos, sin)
    k_all = jnp.concatenate([kc, kl], axis=1)
    v_all = jnp.concatenate([vc, vl], axis=1)
    yb_l = block_attention(q_lat, k_all, v_all).reshape(b, s, MLA_HEADS * V_HEAD) @ p["w_o_mla"]
    yc_l, yc_c = rec_branch(rl, rc, p, ctx_out)
    out_l = gated_merge(gl, ya_l, yb_l, yc_l, p)
    out_c = None
    if ctx_out:
        ya_c = conv_branch(cc, p)
        yb_c = attend(mla_q(qc, p), kc, vc).reshape(b, CTX_LEN, MLA_HEADS * V_HEAD) @ p["w_o_mla"]
        out_c = gated_merge(gc, ya_c, yb_c, yc_c, p)
    return out_l, out_c


def swiglu(h, wg, wu, wd):
    return (jax.nn.silu(h @ wg) * (h @ wu)) @ wd


def moe_swiglu(h, w_router, w_g, w_u, w_d):
    logits = (h @ w_router).astype(jnp.float32)
    top_v, top_i = lax.top_k(logits, TOP_K)
    top_w = jax.nn.softmax(top_v, axis=-1)
    comb = jnp.sum(jax.nn.one_hot(top_i, N_EXPERTS, dtype=jnp.float32) * top_w[..., None], axis=-2)
    out = jnp.zeros_like(h)
    for e in range(N_EXPERTS):
        out = out + comb[..., e:e + 1].astype(h.dtype) * swiglu(h, w_g[e], w_u[e], w_d[e])
    return out


def setup_inputs(seed: int = 0) -> dict:
    key = jax.random.key(seed)
    ks = iter(jax.random.split(key, 48))

    def nrm(shape, fan_in):
        return jax.random.normal(next(ks), shape, jnp.float32) * (fan_in ** -0.5)

    def gain(shape):
        return 1.0 + 0.02 * jax.random.normal(next(ks), shape, jnp.float32)

    def bias(shape):
        return 0.02 * jax.random.normal(next(ks), shape, jnp.float32)

    d = D_MODEL
    inp = {}
    inp["x"] = jax.random.normal(next(ks), (BATCH, SEQ, d), jnp.float32)
    inp["c"] = jax.random.normal(next(ks), (BATCH, d), jnp.float32)
    inp["ctx"] = jax.random.normal(next(ks), (BATCH, CTX_LEN, d), jnp.float32)
    inp["c_ctx"] = jax.random.normal(next(ks), (d,), jnp.float32)
    inp["w_ada"] = 0.5 * nrm((DEPTH, d, 6 * d), d)
    inp["b_ada"] = bias((DEPTH, 6 * d))
    inp["g_norm1"] = gain((DEPTH, d))
    inp["g_norm2"] = gain((DEPTH, d))
    inp["w_in"] = nrm((DEPTH, d, IN_COLS), d)
    inp["b_gate"] = bias((DEPTH, COL_GATE))
    inp["conv_w"] = nrm((DEPTH, CONV_WIDTH, CONV_CH), CONV_WIDTH)
    inp["conv_b"] = bias((DEPTH, CONV_CH))
    inp["conv_ln_g"] = gain((DEPTH, CONV_CH))
    inp["conv_ln_b"] = bias((DEPTH, CONV_CH))
    inp["w_o_conv"] = nrm((DEPTH, CONV_CH, d), CONV_CH)
    inp["g_q_a"] = gain((DEPTH, Q_LORA))
    inp["w_q_b"] = nrm((DEPTH, Q_LORA, MLA_HEADS * QK_HEAD), Q_LORA)
    inp["g_kv_a"] = gain((DEPTH, KV_LORA))
    inp["w_kv_b"] = nrm((DEPTH, KV_LORA, MLA_HEADS * (QK_NOPE + V_HEAD)), KV_LORA)
    inp["g_qn"] = gain((DEPTH, QK_HEAD))
    inp["g_kn"] = gain((DEPTH, QK_HEAD))
    inp["w_o_mla"] = nrm((DEPTH, MLA_HEADS * V_HEAD, d), MLA_HEADS * V_HEAD)
    inp["rec_conv_w"] = nrm((DEPTH, 2, REC_CONV, REC_WIDTH), REC_CONV)
    inp["rec_conv_b"] = bias((DEPTH, 2, REC_WIDTH))
    inp["w_ra"] = nrm((DEPTH, 2, REC_BLOCKS, REC_BW, REC_BW), REC_BW)
    inp["b_ra"] = bias((DEPTH, 2, REC_WIDTH))
    inp["w_ri"] = nrm((DEPTH, 2, REC_BLOCKS, REC_BW, REC_BW), REC_BW)
    inp["b_ri"] = bias((DEPTH, 2, REC_WIDTH))
    u = jax.random.uniform(next(ks), (DEPTH, 2, REC_WIDTH), jnp.float32, minval=0.9, maxval=0.999)
    inp["lru_lambda"] = jnp.log(u) - jnp.log1p(-u)
    inp["w_o_rec"] = nrm((DEPTH, REC_WIDTH, d), REC_WIDTH)
    inp["w_out"] = nrm((DEPTH, d, d), d)
    inp["w_ff_gate"] = nrm((N_DENSE, d, D_FF), d)
    inp["w_ff_up"] = nrm((N_DENSE, d, D_FF), d)
    inp["w_ff_down"] = nrm((N_DENSE, D_FF, d), D_FF)
    inp["w_router"] = nrm((N_MOE, d, N_EXPERTS), d)
    inp["w_e_gate"] = nrm((N_MOE, N_EXPERTS, d, D_FF), d)
    inp["w_e_up"] = nrm((N_MOE, N_EXPERTS, d, D_FF), d)
    inp["w_e_down"] = nrm((N_MOE, N_EXPERTS, D_FF, d), D_FF)
    return inp


def reference(x, c, ctx, c_ctx, w_ada, b_ada, g_norm1, g_norm2, w_in, b_gate, conv_w, conv_b, conv_ln_g,
              conv_ln_b, w_o_conv, g_q_a, w_q_b, g_kv_a, w_kv_b, g_qn, g_kn, w_o_mla, rec_conv_w, rec_conv_b,
              w_ra, b_ra, w_ri, b_ri, lru_lambda, w_o_rec, w_out, w_ff_gate, w_ff_up, w_ff_down, w_router,
              w_e_gate, w_e_up, w_e_down):
    seq = x.shape[1]
    cos, sin = axial_rope_tables(seq)
    silu_c = jax.nn.silu(c)
    silu_cc = jax.nn.silu(c_ctx)
    for i in range(DEPTH):
        last = i == DEPTH - 1
        m_l = (silu_c @ w_ada[i] + b_ada[i])[:, None, :]
        m_c = silu_cc @ w_ada[i] + b_ada[i]
        sh1, sc1, ga1, sh2, sc2, ga2 = jnp.split(m_l, 6, axis=-1)
        csh1, csc1, cga1, csh2, csc2, cga2 = jnp.split(m_c, 6, axis=-1)
        p = {"w_in": w_in[i], "b_gate": b_gate[i], "conv_w": conv_w[i], "conv_b": conv_b[i],
             "conv_ln_g": conv_ln_g[i], "conv_ln_b": conv_ln_b[i], "w_o_conv": w_o_conv[i],
             "g_q_a": g_q_a[i], "w_q_b": w_q_b[i], "g_kv_a": g_kv_a[i], "w_kv_b": w_kv_b[i],
             "g_qn": g_qn[i], "g_kn": g_kn[i], "w_o_mla": w_o_mla[i], "rec_conv_w": rec_conv_w[i],
             "rec_conv_b": rec_conv_b[i], "w_ra": w_ra[i], "b_ra": b_ra[i], "w_ri": w_ri[i],
             "b_ri": b_ri[i], "lru_lambda": lru_lambda[i], "w_o_rec": w_o_rec[i], "w_out": w_out[i]}
        hl = modulate(rmsnorm(x, g_norm1[i]), sh1, sc1)
        hc = modulate(rmsnorm(ctx, g_norm1[i]), csh1, csc1)
        yl, yc = mixer(hl, hc, p, cos, sin, ctx_out=not last)
        x = x + ga1 * yl
        if not last:
            ctx = ctx + cga1 * yc
        hl = modulate(rmsnorm(x, g_norm2[i]), sh2, sc2)
        if last:
            h = hl
        else:
            h = jnp.concatenate([modulate(rmsnorm(ctx, g_norm2[i]), csh2, csc2), hl], axis=1)
        j = i // 2
        if i % 2 == 0:
            f = swiglu(h, w_ff_gate[j], w_ff_up[j], w_ff_down[j])
        else:
            f = moe_swiglu(h, w_router[j], w_e_gate[j], w_e_up[j], w_e_down[j])
        if last:
            x = x + ga2 * f
        else:
            x = x + ga2 * f[:, CTX_LEN:]
            ctx = ctx + cga2 * f[:, :CTX_LEN]
    return x
```

```python
import functools
import math

import jax
import jax.numpy as jnp
from jax import lax
from jax.experimental import pallas as pl
from jax.experimental.pallas import tpu as pltpu

F32 = jnp.float32
BF16 = jnp.bfloat16

NORM_EPS = 1e-6
GRID_W = 64
CONV_CH = 512
CONV_WIDTH = 31
MLA_HEADS = 8
QK_NOPE = 64
QK_ROPE = 32
V_HEAD = 64
Q_LORA = 768
KV_LORA = 256
QK_HEAD = QK_NOPE + QK_ROPE
REC_WIDTH = 512
REC_CONV = 4
LRU_C = 8.0
N_BRANCH = 3
TOP_K = 2
ROPE_BASE = 10000.0

LANES = 128
SUBLANES = 8
HEAD_PAD = LANES
KV_IN_PAD = 384
ROW_TILE = 256
FFN_CHUNK = 512
MOE_TILE = 512
VMEM_LIMIT = 56 * 1024 * 1024


def _cparams(*sem):
    return pltpu.CompilerParams(dimension_semantics=sem, vmem_limit_bytes=VMEM_LIMIT)


def _const_spec(shape):
    nd = len(shape)
    return pl.BlockSpec(shape, lambda *_: (0,) * nd)


def _sigmoid(x):
    return 1.0 / (1.0 + jnp.exp(-x))


def _silu(x):
    return x * _sigmoid(x)


def _mod_row(ml_ref, mc_ref, k, is_ctx):
    return jnp.where(is_ctx, mc_ref[0, k:k + 1, :], ml_ref[0, k:k + 1, :])


def _is_ctx(t, tm, ctx_len):
    row = t * tm + lax.broadcasted_iota(jnp.int32, (tm, 1), 0)
    return row < ctx_len


def _rms_mod(x, g, shift, scale):
    y = x * lax.rsqrt(jnp.mean(x * x, axis=-1, keepdims=True) + NORM_EPS) * g
    return y * (1.0 + scale) + shift


def _ada_body(c_ref, w_ref, b_ref, o_ref):
    c = c_ref[...]
    s = _silu(c).astype(BF16)
    o_ref[0] = jnp.dot(s, w_ref[0].astype(BF16), preferred_element_type=F32) + b_ref[0]


def _ada_call(cpad, w_ada, b_ada):
    depth, d, n = w_ada.shape
    r = cpad.shape[0]
    tn = 1536
    return pl.pallas_call(
        _ada_body,
        grid=(depth, n // tn),
        in_specs=[
            pl.BlockSpec((r, d), lambda l, j: (0, 0)),
            pl.BlockSpec((1, d, tn), lambda l, j: (l, 0, j)),
            pl.BlockSpec((1, 1, tn), lambda l, j: (l, 0, j)),
        ],
        out_specs=pl.BlockSpec((1, r, tn), lambda l, j: (l, 0, j)),
        out_shape=jax.ShapeDtypeStruct((depth, r, n), F32),
        compiler_params=_cparams("arbitrary", "arbitrary"),
        name="ada",
    )(cpad, w_ada, b_ada.reshape(depth, 1, n))


_SEG_WIDTHS = (2 * CONV_CH, Q_LORA, KV_IN_PAD, 2 * REC_WIDTH, N_BRANCH * 1024)
_DOT_COLS = 512


def _inproj_body(x_ref, ml_ref, mc_ref, g_ref, w_ref, *out_refs, tm, ctx_len):
    is_ctx = _is_ctx(pl.program_id(1), tm, ctx_len)
    h = _rms_mod(x_ref[...], g_ref[...], _mod_row(ml_ref, mc_ref, 0, is_ctx), _mod_row(ml_ref, mc_ref, 1, is_ctx))
    hb = h.astype(BF16)
    c0 = 0
    for ref, width in zip(out_refs, _SEG_WIDTHS):
        for j in range(0, width, _DOT_COLS):
            cw = min(_DOT_COLS, width - j)
            ref[:, j:j + cw] = jnp.dot(hb, w_ref[:, c0 + j:c0 + j + cw], preferred_element_type=F32).astype(BF16)
        c0 += width


def _inproj_call(x, ml, mc, g, w, *, nb, t_len, ctx_len):
    m, d = x.shape
    tm = ROW_TILE
    nt = t_len // tm
    row = lambda b, t: (b * nt + t, 0)
    return pl.pallas_call(
        functools.partial(_inproj_body, tm=tm, ctx_len=ctx_len),
        grid=(nb, nt),
        in_specs=[
            pl.BlockSpec((tm, d), row),
            pl.BlockSpec((1, 6, d), lambda b, t: (b, 0, 0)),
            _const_spec((1, 6, d)),
            _const_spec((1, d)),
            pl.BlockSpec(w.shape, lambda b, t: (0, 0), pipeline_mode=pl.Buffered(1)),
        ],
        out_specs=[pl.BlockSpec((tm, wd), row) for wd in _SEG_WIDTHS],
        out_shape=[jax.ShapeDtypeStruct((m, wd), BF16) for wd in _SEG_WIDTHS],
        compiler_params=_cparams("parallel", "parallel"),
        name="inproj",
    )(x, ml, mc, g, w)


_CONV_PAD = 16
_CONV_ROWS = 64


def _conv_body(z_ref, w_ref, b_ref, g_ref, bb_ref, o_ref, u_ref, c_ref, *, t_len, ctx_len):
    ch, pad, rc = CONV_CH, _CONV_PAD, _CONV_ROWS
    zeros = jnp.zeros((pad, ch), F32)
    u_ref[0:pad] = zeros
    u_ref[pad + ctx_len:2 * pad + ctx_len] = zeros
    u_ref[2 * pad + t_len:3 * pad + t_len] = zeros

    def u_row(r0):
        return pl.multiple_of(r0 + pad + jnp.where(r0 >= ctx_len, pad, 0), SUBLANES)

    def glu(i, carry):
        r0 = pl.multiple_of(i * rc, rc)
        z = z_ref[pl.ds(r0, rc), :].astype(F32)
        u_ref[pl.ds(u_row(r0), rc), :] = z[:, :ch] * _sigmoid(z[:, ch:])
        return carry

    lax.fori_loop(0, t_len // rc, glu, 0)

    win = rc + 2 * pad

    def chunk(i, carry):
        r0 = pl.multiple_of(i * rc, rc)
        base = pl.multiple_of(u_row(r0) - pad, SUBLANES)
        for cb in range(ch // LANES):
            ls = slice(cb * LANES, (cb + 1) * LANES)
            w = u_ref[pl.ds(base, win), ls]
            acc = jnp.zeros((rc, LANES), F32)
            for b in range(SUBLANES):
                wb = w if b == 0 else pltpu.roll(w, win - b, axis=0)
                for a in range(win // SUBLANES):
                    k = SUBLANES * a + b - (pad - CONV_WIDTH // 2)
                    if 0 <= k < CONV_WIDTH:
                        acc = acc + w_ref[k:k + 1, ls] * wb[SUBLANES * a:SUBLANES * a + rc]
            c_ref[:, ls] = acc + b_ref[:, ls]
        v = c_ref[...]
        mu = jnp.mean(v, axis=-1, keepdims=True)
        vc = v - mu
        var = jnp.mean(vc * vc, axis=-1, keepdims=True)
        y = vc * lax.rsqrt(var + NORM_EPS) * g_ref[...] + bb_ref[...]
        o_ref[pl.ds(r0, rc), :] = _silu(y).astype(BF16)
        return carry

    lax.fori_loop(0, t_len // rc, chunk, 0)


def _conv_call(zc, w, b, g, bb, *, nb, t_len, ctx_len):
    m = zc.shape[0]
    ch = CONV_CH
    return pl.pallas_call(
        functools.partial(_conv_body, t_len=t_len, ctx_len=ctx_len),
        grid=(nb,),
        in_specs=[
            pl.BlockSpec((t_len, 2 * ch), lambda i: (i, 0)),
            _const_spec((CONV_WIDTH, ch)),
            _const_spec((1, ch)),
            _const_spec((1, ch)),
            _const_spec((1, ch)),
        ],
        out_specs=pl.BlockSpec((t_len, ch), lambda i: (i, 0)),
        out_shape=jax.ShapeDtypeStruct((m, ch), BF16),
        scratch_shapes=[
            pltpu.VMEM((t_len + 3 * _CONV_PAD, ch), F32),
            pltpu.VMEM((_CONV_ROWS, ch), F32),
        ],
        compiler_params=_cparams("parallel"),
        name="conv",
    )(zc, w, b, g, bb)


_REC_ROWS = 128
_REC_PAD = SUBLANES


def _rec_body(z_ref, cw_ref, cb_ref, wa_ref, ba_ref, wi_ref, bi_ref, lam_ref, o_ref, xp_ref, hf_ref, *, t_len, ctx_len):
    wd, ch, pad = REC_WIDTH, _REC_ROWS, _REC_PAD
    nch = t_len // ch
    nc_ctx = ctx_len // ch
    nblk = ch // SUBLANES
    zeros = jnp.zeros((pad, wd), F32)
    xp_ref[0:pad] = zeros
    xp_ref[pad + ctx_len:2 * pad + ctx_len] = zeros
    xp_ref[2 * pad + t_len:3 * pad + t_len] = zeros

    def xp_row(i):
        return pl.multiple_of(i * ch + pad + jnp.where(i >= nc_ctx, pad, 0), SUBLANES)

    def fill(i, carry):
        r0 = pl.multiple_of(i * ch, ch)
        xp_ref[pl.ds(xp_row(i), ch), :] = z_ref[pl.ds(r0, ch), 0:wd].astype(F32)
        return carry

    lax.fori_loop(0, nch, fill, 0)

    row8 = lax.broadcasted_iota(jnp.int32, (ch, 1), 0) & (SUBLANES - 1)
    win = ch + pad

    def conv4(i, d):
        if d == 0:
            w = xp_ref[pl.ds(xp_row(i) - pad, win), :]
            shifts = [pad - (REC_CONV - 1) + k for k in range(REC_CONV)]
        else:
            w = xp_ref[pl.ds(xp_row(i), win), :]
            shifts = list(range(REC_CONV))
        acc = jnp.zeros((ch, wd), F32) + cb_ref[d]
        taps = cw_ref[d]
        for k, s in enumerate(shifts):
            if s % SUBLANES == 0:
                ws = w[s:s + ch]
            else:
                ws = pltpu.roll(w, win - s, axis=0)[0:ch]
            acc = acc + taps[k:k + 1, :] * ws
        return acc

    def gates(xc, d):
        xb = xc.astype(BF16)
        r = _sigmoid(jnp.dot(xb, wa_ref[d], preferred_element_type=F32) + ba_ref[d])
        ig = _sigmoid(jnp.dot(xb, wi_ref[d], preferred_element_type=F32) + bi_ref[d])
        lam = lam_ref[d]
        softplus_neg = jnp.maximum(-lam, 0.0) + jnp.log1p(jnp.exp(-jnp.abs(lam)))
        log_a = -LRU_C * r * softplus_neg
        a = jnp.exp(log_a)
        bx = jnp.sqrt(-jnp.tanh(log_a) * (1.0 + a * a)) * (ig * xc)
        return a, bx

    def scan_chunk(a, b, carry, d):
        for s in (1, 2, 4):
            if d == 0:
                a_s, b_s, keep = pltpu.roll(a, s, axis=0), pltpu.roll(b, s, axis=0), row8 >= s
            else:
                a_s, b_s, keep = pltpu.roll(a, ch - s, axis=0), pltpu.roll(b, ch - s, axis=0), row8 < SUBLANES - s
            b = jnp.where(keep, a * b_s + b, b)
            a = jnp.where(keep, a * a_s, a)
        outs = [None] * nblk
        for j in (range(nblk) if d == 0 else reversed(range(nblk))):
            sl = slice(j * SUBLANES, (j + 1) * SUBLANES)
            hj = b[sl] + a[sl] * carry
            outs[j] = hj
            carry = hj[SUBLANES - 1:SUBLANES] if d == 0 else hj[0:1]
        return jnp.concatenate(outs, axis=0), carry

    def fwd(i, carry):
        a, bx = gates(conv4(i, 0), 0)
        h, carry = scan_chunk(a, bx, carry, 0)
        hf_ref[pl.ds(pl.multiple_of(i * ch, ch), ch), :] = h
        return carry

    lax.fori_loop(0, nch, fwd, jnp.zeros((1, wd), F32))

    def bwd(j, carry):
        i = jnp.where(j < nc_ctx, nc_ctx - 1 - j, nch - 1 - (j - nc_ctx))
        a, bx = gates(conv4(i, 1), 1)
        h, carry = scan_chunk(a, bx, carry, 1)
        r0 = pl.multiple_of(i * ch, ch)
        gate = z_ref[pl.ds(r0, ch), wd:2 * wd].astype(F32)
        hsum = hf_ref[pl.ds(r0, ch), :] + h
        o_ref[pl.ds(r0, ch), :] = (hsum * jax.nn.gelu(gate)).astype(BF16)
        return carry

    lax.fori_loop(0, nch, bwd, jnp.zeros((1, wd), F32))


def _rec_call(zr, cw, cb, wa, ba, wi, bi, lam, *, nb, t_len, ctx_len):
    m = zr.shape[0]
    wd = REC_WIDTH
    return pl.pallas_call(
        functools.partial(_rec_body, t_len=t_len, ctx_len=ctx_len),
        grid=(nb,),
        in_specs=[
            pl.BlockSpec((t_len, 2 * wd), lambda i: (i, 0)),
            _const_spec((2, REC_CONV, wd)),
            _const_spec((2, 1, wd)),
            _const_spec((2, wd, wd)),
            _const_spec((2, 1, wd)),
            _const_spec((2, wd, wd)),
            _const_spec((2, 1, wd)),
            _const_spec((2, 1, wd)),
        ],
        out_specs=pl.BlockSpec((t_len, wd), lambda i: (i, 0)),
        out_shape=jax.ShapeDtypeStruct((m, wd), BF16),
        scratch_shapes=[
            pltpu.VMEM((t_len + 3 * _REC_PAD, wd), F32),
            pltpu.VMEM((t_len, wd), F32),
        ],
        compiler_params=_cparams("parallel"),
        name="rglru",
    )(zr, cw, cb, wa, ba, wi, bi, lam)


def _qkv_body(zq_ref, zkv_ref, ta_ref, tb_ref, gqa_ref, gkva_ref, wq_ref, wkv_ref, gq_ref, gk_ref, q_ref, k_ref, v_ref):
    hp = HEAD_PAD
    zq = zq_ref[...].astype(F32)
    qa = zq * lax.rsqrt(jnp.mean(zq * zq, axis=-1, keepdims=True) + NORM_EPS) * gqa_ref[...]
    q = jnp.dot(qa.astype(BF16), wq_ref[...], preferred_element_type=F32)

    zkv = zkv_ref[...].astype(F32)
    lora = lax.broadcasted_iota(jnp.int32, (1, KV_IN_PAD), 1) < KV_LORA
    ms = jnp.sum(jnp.where(lora, zkv * zkv, 0.0), axis=-1, keepdims=True) * (1.0 / KV_LORA)
    lhs = jnp.where(lora, zkv * lax.rsqrt(ms + NORM_EPS) * gkva_ref[...], zkv)
    kv = jnp.dot(lhs.astype(BF16), wkv_ref[...], preferred_element_type=F32)
    v_ref[...] = kv[:, MLA_HEADS * hp:].astype(BF16)

    ta, tb = ta_ref[...], tb_ref[...]
    scale = 1.0 / math.sqrt(QK_HEAD)
    q_a, q_b = ta * gq_ref[0:1, :] * scale, tb * gq_ref[1:2, :] * scale
    k_a, k_b = ta * gk_ref[0:1, :], tb * gk_ref[1:2, :]
    real = lax.broadcasted_iota(jnp.int32, (1, hp), 1) < QK_HEAD
    for h in range(MLA_HEADS):
        sl = slice(h * hp, (h + 1) * hp)
        for src, fa, fb, ref in ((q, q_a, q_b, q_ref), (kv, k_a, k_b, k_ref)):
            t = src[:, sl]
            ss = jnp.sum(jnp.where(real, t * t, 0.0), axis=-1, keepdims=True)
            rs = lax.rsqrt(ss * (1.0 / QK_HEAD) + NORM_EPS)
            ref[:, sl] = (rs * (t * fa + pltpu.roll(t, hp - QK_ROPE, axis=1) * fb)).astype(BF16)


def _qkv_call(zq, zkv, ta, tb, gqa, gkva, wq, wkv, gq, gk, *, nb, t_len):
    m = zq.shape[0]
    tm = ROW_TILE
    nt = t_len // tm
    row = lambda b, t: (b * nt + t, 0)
    pos = lambda b, t: (t, 0)
    nq = MLA_HEADS * HEAD_PAD
    nv = MLA_HEADS * V_HEAD
    return pl.pallas_call(
        _qkv_body,
        grid=(nb, nt),
        in_specs=[
            pl.BlockSpec((tm, Q_LORA), row),
            pl.BlockSpec((tm, KV_IN_PAD), row),
            pl.BlockSpec((tm, HEAD_PAD), pos),
            pl.BlockSpec((tm, HEAD_PAD), pos),
            _const_spec((1, Q_LORA)),
            _const_spec((1, KV_IN_PAD)),
            _const_spec(wq.shape),
            _const_spec(wkv.shape),
            _const_spec((2, HEAD_PAD)),
            _const_spec((2, HEAD_PAD)),
        ],
        out_specs=[pl.BlockSpec((tm, nq), row), pl.BlockSpec((tm, nq), row), pl.BlockSpec((tm, nv), row)],
        out_shape=[jax.ShapeDtypeStruct((m, nq), BF16), jax.ShapeDtypeStruct((m, nq), BF16),
                   jax.ShapeDtypeStruct((m, nv), BF16)],
        compiler_params=_cparams("parallel", "parallel"),
        name="qkv",
    )(zq, zkv, ta, tb, gqa, gkva, wq, wkv, gq, gk)


def _attn_body(q_ref, k_ref, v_ref, o_ref, *, t_len, ctx_len):
    hp = HEAD_PAD
    low = lax.broadcasted_iota(jnp.int32, (1, 2 * V_HEAD), 1) < V_HEAD

    def attend(nk):
        for pair in range(MLA_HEADS // 2):
            vs = v_ref[0:nk, pair * 2 * V_HEAD:(pair + 1) * 2 * V_HEAD]
            outs = []
            for h in (2 * pair, 2 * pair + 1):
                q = q_ref[:, h * hp:(h + 1) * hp]
                k = k_ref[0:nk, h * hp:(h + 1) * hp]
                s = lax.dot_general(q, k, (((1,), (1,)), ((), ())), preferred_element_type=F32)
                p = jnp.exp(s - jnp.max(s, axis=-1, keepdims=True))
                l = jnp.sum(p, axis=-1, keepdims=True)
                outs.append(jnp.dot(p.astype(BF16), vs, preferred_element_type=F32) / l)
            o_ref[:, pair * 2 * V_HEAD:(pair + 1) * 2 * V_HEAD] = jnp.where(low, outs[0], outs[1]).astype(BF16)

    is_ctx_tile = pl.program_id(1) * ROW_TILE < ctx_len

    @pl.when(is_ctx_tile)
    def _():
        attend(ctx_len)

    @pl.when(jnp.logical_not(is_ctx_tile))
    def _():
        attend(t_len)


def _attn_call(q, k, v, *, nb, t_len, ctx_len):
    m = q.shape[0]
    tq = ROW_TILE
    nt = t_len // tq
    nq = MLA_HEADS * HEAD_PAD
    nv = MLA_HEADS * V_HEAD
    return pl.pallas_call(
        functools.partial(_attn_body, t_len=t_len, ctx_len=ctx_len),
        grid=(nb, nt),
        in_specs=[
            pl.BlockSpec((tq, nq), lambda b, t: (b * nt + t, 0)),
            pl.BlockSpec((t_len, nq), lambda b, t: (b, 0)),
            pl.BlockSpec((t_len, nv), lambda b, t: (b, 0)),
        ],
        out_specs=pl.BlockSpec((tq, nv), lambda b, t: (b * nt + t, 0)),
        out_shape=jax.ShapeDtypeStruct((m, nv), BF16),
        compiler_params=_cparams("parallel", "arbitrary"),
        name="attn",
    )(q, k, v)


def _merge_body(ca_ref, ao_ref, rr_ref, zg_ref, x_ref, ml_ref, mc_ref, bg_ref, g2_ref, wc_ref, wm_ref, wr_ref,
                wo_ref, *rest, tm, ctx_len, n_exp):
    moe = n_exp > 0
    d = x_ref.shape[-1]
    is_ctx = _is_ctx(pl.program_id(1), tm, ctx_len)
    merged = jnp.zeros((tm, d), F32)
    for j, (src, w) in enumerate(((ca_ref, wc_ref), (ao_ref, wm_ref), (rr_ref, wr_ref))):
        gate = _sigmoid(zg_ref[:, j * d:(j + 1) * d].astype(F32) + bg_ref[:, j * d:(j + 1) * d])
        merged = merged + gate * jnp.dot(src[...], w[...], preferred_element_type=F32)
    y = jnp.dot(merged.astype(BF16), wo_ref[...], preferred_element_type=F32)
    x = x_ref[...] + _mod_row(ml_ref, mc_ref, 2, is_ctx) * y
    h2 = _rms_mod(x, g2_ref[...], _mod_row(ml_ref, mc_ref, 3, is_ctx), _mod_row(ml_ref, mc_ref, 4, is_ctx))
    if not moe:
        xo_ref, h_ref = rest
        xo_ref[...] = x
        h_ref[...] = h2.astype(BF16)
        return
    wrt_ref, xo_ref, h_ref, ei_ref, ew_ref = rest
    xo_ref[...] = x
    for s in range(d // LANES):
        h_ref[pl.ds(s, tm, stride=SUBLANES), :] = h2[:, s * LANES:(s + 1) * LANES]
    logits = jnp.dot(h2, wrt_ref[...], precision=lax.Precision.HIGHEST, preferred_element_type=F32)
    lane = lax.broadcasted_iota(jnp.int32, (tm, LANES), 1).astype(F32)
    logits = jnp.where(lane < n_exp, logits, -jnp.inf)
    m1 = jnp.max(logits, axis=-1, keepdims=True)
    i1 = jnp.min(jnp.where(logits == m1, lane, float(LANES)), axis=-1, keepdims=True)
    rest_l = jnp.where(lane == i1, -jnp.inf, logits)
    m2 = jnp.max(rest_l, axis=-1, keepdims=True)
    i2 = jnp.min(jnp.where(rest_l == m2, lane, float(LANES)), axis=-1, keepdims=True)
    e2 = jnp.exp(m2 - m1)
    w1 = 1.0 / (1.0 + e2)
    w2 = e2 / (1.0 + e2)
    ei_ref[...] = jnp.where(lane == 0.0, i1, jnp.where(lane == 1.0, i2, 0.0)).astype(jnp.int32)
    ew_ref[...] = jnp.where(lane == 0.0, w1, jnp.where(lane == 1.0, w2, 0.0))


def _merge_call(ca, ao, rr, zg, x, ml, mc, bg, g2, wc, wm, wr, wo, w_router, *, nb, t_len, ctx_len):
    m, d = x.shape
    tm = ROW_TILE
    nt = t_len // tm
    row = lambda b, t: (b * nt + t, 0)
    moe = w_router is not None
    n_exp = w_router.shape[1] if moe else 0
    wrt = jnp.pad(w_router, ((0, 0), (0, LANES - n_exp))) if moe else None
    in_specs = [
        pl.BlockSpec((tm, ca.shape[1]), row),
        pl.BlockSpec((tm, ao.shape[1]), row),
        pl.BlockSpec((tm, rr.shape[1]), row),
        pl.BlockSpec((tm, zg.shape[1]), row),
        pl.BlockSpec((tm, d), row),
        pl.BlockSpec((1, 6, d), lambda b, t: (b, 0, 0)),
        _const_spec((1, 6, d)),
        _const_spec(bg.shape),
        _const_spec(g2.shape),
        _const_spec(wc.shape),
        _const_spec(wm.shape),
        _const_spec(wr.shape),
        _const_spec(wo.shape),
    ]
    args = [ca, ao, rr, zg, x, ml, mc, bg, g2, wc, wm, wr, wo]
    out_specs = [pl.BlockSpec((tm, d), row)]
    out_shape = [jax.ShapeDtypeStruct((m, d), F32)]
    if moe:
        in_specs.append(_const_spec(wrt.shape))
        args.append(wrt)
        out_specs += [pl.BlockSpec((tm * SUBLANES, LANES), row), pl.BlockSpec((tm, LANES), row),
                      pl.BlockSpec((tm, LANES), row)]
        out_shape += [jax.ShapeDtypeStruct((m * SUBLANES, LANES), F32), jax.ShapeDtypeStruct((m, LANES), jnp.int32),
                      jax.ShapeDtypeStruct((m, LANES), F32)]
    else:
        out_specs.append(pl.BlockSpec((tm, d), row))
        out_shape.append(jax.ShapeDtypeStruct((m, d), BF16))
    return pl.pallas_call(
        functools.partial(_merge_body, tm=tm, ctx_len=ctx_len, n_exp=n_exp),
        grid=(nb, nt),
        in_specs=in_specs,
        out_specs=out_specs,
        out_shape=out_shape,
        input_output_aliases={4: 0},
        compiler_params=_cparams("parallel", "parallel"),
        name="merge_moe" if moe else "merge",
    )(*args)


def _ffn_body(h_ref, x_ref, ml_ref, mc_ref, wg_ref, wu_ref, wd_ref, o_ref, acc_ref, *, tm, ctx_len):
    f = pl.program_id(2)

    @pl.when(f == 0)
    def _():
        acc_ref[...] = jnp.zeros_like(acc_ref)

    h = h_ref[...]
    g = jnp.dot(h, wg_ref[...], preferred_element_type=F32)
    u = jnp.dot(h, wu_ref[...], preferred_element_type=F32)
    acc_ref[...] += jnp.dot((_silu(g) * u).astype(BF16), wd_ref[...], preferred_element_type=F32)

    @pl.when(f == pl.num_programs(2) - 1)
    def _():
        is_ctx = _is_ctx(pl.program_id(1), tm, ctx_len)
        o_ref[...] = x_ref[...] + _mod_row(ml_ref, mc_ref, 5, is_ctx) * acc_ref[...]


def _ffn_call(h, x, ml, mc, wg, wu, wd, *, nb, t_len, ctx_len):
    m, d = x.shape
    dff = wg.shape[1]
    tm = 768 if t_len % 768 == 0 else ROW_TILE
    nt = t_len // tm
    tf = FFN_CHUNK
    row = lambda b, t, f: (b * nt + t, 0)
    return pl.pallas_call(
        functools.partial(_ffn_body, tm=tm, ctx_len=ctx_len),
        grid=(nb, nt, dff // tf),
        in_specs=[
            pl.BlockSpec((tm, d), row),
            pl.BlockSpec((tm, d), row),
            pl.BlockSpec((1, 6, d), lambda b, t, f: (b, 0, 0)),
            pl.BlockSpec((1, 6, d), lambda b, t, f: (0, 0, 0)),
            pl.BlockSpec((d, tf), lambda b, t, f: (0, f)),
            pl.BlockSpec((d, tf), lambda b, t, f: (0, f)),
            pl.BlockSpec((tf, d), lambda b, t, f: (f, 0)),
        ],
        out_specs=pl.BlockSpec((tm, d), row),
        out_shape=jax.ShapeDtypeStruct((m, d), F32),
        scratch_shapes=[pltpu.VMEM((tm, d), F32)],
        input_output_aliases={1: 0},
        compiler_params=_cparams("parallel", "parallel", "arbitrary"),
        name="ffn",
    )(h, x, ml, mc, wg, wu, wd)


def _moe_body(te_ref, nu_ref, st_ref, h_hbm, wg_ref, wu_ref, wd_ref, y_ref, xt_ref, xb_ref, acc_ref, sem, *, tm):
    i, f = pl.program_id(0), pl.program_id(1)
    nf = pl.num_programs(1)
    d = xb_ref.shape[1]
    valid = i < nu_ref[0]

    def token_copy(tok, r):
        return pltpu.make_async_copy(
            h_hbm.at[pl.ds(pl.multiple_of(tok * SUBLANES, SUBLANES), SUBLANES), :],
            xt_ref.at[pl.ds(pl.multiple_of(r * SUBLANES, SUBLANES), SUBLANES), :], sem)

    @pl.when(jnp.logical_and(valid, f == 0))
    def _():
        def issue(r, carry):
            token_copy(st_ref[0, 0, r], r).start()
            return carry

        lax.fori_loop(0, tm, issue, 0)

        def drain(r, carry):
            token_copy(0, r).wait()
            return carry

        lax.fori_loop(0, tm, drain, 0)
        for s in range(d // LANES):
            xb_ref[:, s * LANES:(s + 1) * LANES] = xt_ref[pl.ds(s, tm, stride=SUBLANES), :].astype(BF16)
        acc_ref[...] = jnp.zeros_like(acc_ref)

    @pl.when(valid)
    def _():
        x = xb_ref[...]
        g = jnp.dot(x, wg_ref[0], preferred_element_type=F32)
        u = jnp.dot(x, wu_ref[0], preferred_element_type=F32)
        acc_ref[...] += jnp.dot((_silu(g) * u).astype(BF16), wd_ref[0], preferred_element_type=F32)

    @pl.when(jnp.logical_and(valid, f == nf - 1))
    def _():
        for s in range(d // LANES):
            y_ref[pl.ds(s, tm, stride=SUBLANES), :] = acc_ref[:, s * LANES:(s + 1) * LANES]

    @pl.when(jnp.logical_and(jnp.logical_not(valid), f == nf - 1))
    def _():
        y_ref[...] = jnp.zeros_like(y_ref)


def _moe_call(tile_expert, n_used, slot_token, h_tiles, wg, wu, wd):
    n_exp, d, dff = wg.shape
    tm = MOE_TILE
    n_tiles = slot_token.shape[0]
    tf = FFN_CHUNK
    nf = dff // tf

    def w_in_map(i, f, te, nu):
        return (te[i], 0, jnp.where(i < nu[0], f, nf - 1))

    def w_out_map(i, f, te, nu):
        return (te[i], jnp.where(i < nu[0], f, nf - 1), 0)

    grid_spec = pltpu.PrefetchScalarGridSpec(
        num_scalar_prefetch=2,
        grid=(n_tiles, nf),
        in_specs=[
            pl.BlockSpec((1, 1, tm), lambda i, f, te, nu: (i, 0, 0), memory_space=pltpu.SMEM),
            pl.BlockSpec(memory_space=pl.ANY),
            pl.BlockSpec((1, d, tf), w_in_map),
            pl.BlockSpec((1, d, tf), w_in_map),
            pl.BlockSpec((1, tf, d), w_out_map),
        ],
        out_specs=pl.BlockSpec((tm * SUBLANES, LANES), lambda i, f, te, nu: (i, 0)),
        scratch_shapes=[
            pltpu.VMEM((tm * SUBLANES, LANES), F32),
            pltpu.VMEM((tm, d), BF16),
            pltpu.VMEM((tm, d), F32),
            pltpu.SemaphoreType.DMA(()),
        ],
    )
    return pl.pallas_call(
        functools.partial(_moe_body, tm=tm),
        grid_spec=grid_spec,
        out_shape=jax.ShapeDtypeStruct((n_tiles * tm * SUBLANES, LANES), F32),
        compiler_params=_cparams("arbitrary", "arbitrary"),
        name="moe",
    )(tile_expert, n_used, slot_token, h_tiles, wg, wu, wd)


def _combine_body(p1_ref, p2_ref, y_hbm, ew_ref, x_ref, ml_ref, mc_ref, o_ref, y1_ref, y2_ref, sem, *, tm, ctx_len):
    d = x_ref.shape[1]

    def slot_copy(pos, r, dst):
        return pltpu.make_async_copy(
            y_hbm.at[pl.ds(pl.multiple_of(pos * SUBLANES, SUBLANES), SUBLANES), :],
            dst.at[pl.ds(pl.multiple_of(r * SUBLANES, SUBLANES), SUBLANES), :], sem)

    def issue(r, carry):
        slot_copy(p1_ref[0, 0, r], r, y1_ref).start()
        slot_copy(p2_ref[0, 0, r], r, y2_ref).start()
        return carry

    lax.fori_loop(0, tm, issue, 0)

    def drain(r, carry):
        slot_copy(0, r, y1_ref).wait()
        slot_copy(0, r, y2_ref).wait()
        return carry

    lax.fori_loop(0, tm, drain, 0)
    is_ctx = _is_ctx(pl.program_id(1), tm, ctx_len)
    w1, w2 = ew_ref[:, 0:1], ew_ref[:, 1:2]
    ga = _mod_row(ml_ref, mc_ref, 5, is_ctx)
    for s in range(d // LANES):
        ls = slice(s * LANES, (s + 1) * LANES)
        f = w1 * y1_ref[pl.ds(s, tm, stride=SUBLANES), :] + w2 * y2_ref[pl.ds(s, tm, stride=SUBLANES), :]
        o_ref[:, ls] = x_ref[:, ls] + ga[:, ls] * f


def _combine_call(pos1, pos2, y_tiles, ew, x, ml, mc, *, nb, t_len, ctx_len):
    m, d = x.shape
    tm = ROW_TILE
    nt = t_len // tm
    row = lambda b, t: (b * nt + t, 0)
    tile3 = lambda b, t: (b * nt + t, 0, 0)
    return pl.pallas_call(
        functools.partial(_combine_body, tm=tm, ctx_len=ctx_len),
        grid=(nb, nt),
        in_specs=[
            pl.BlockSpec((1, 1, tm), tile3, memory_space=pltpu.SMEM),
            pl.BlockSpec((1, 1, tm), tile3, memory_space=pltpu.SMEM),
            pl.BlockSpec(memory_space=pl.ANY),
            pl.BlockSpec((tm, LANES), row),
            pl.BlockSpec((tm, d), row),
            pl.BlockSpec((1, 6, d), lambda b, t: (b, 0, 0)),
            _const_spec((1, 6, d)),
        ],
        out_specs=pl.BlockSpec((tm, d), row),
        out_shape=jax.ShapeDtypeStruct((m, d), F32),
        scratch_shapes=[
            pltpu.VMEM((tm * SUBLANES, LANES), F32),
            pltpu.VMEM((tm * SUBLANES, LANES), F32),
            pltpu.SemaphoreType.DMA(()),
        ],
        input_output_aliases={4: 0},
        compiler_params=_cparams("arbitrary", "arbitrary"),
        name="moe_combine",
    )(pos1, pos2, y_tiles, ew, x, ml, mc)


def _route_tables(eidx, n_exp, tile):
    m = eidx.shape[0]
    e = eidx[:, :TOP_K].reshape(-1)
    onehot = (e[:, None] == jnp.arange(n_exp, dtype=jnp.int32)[None, :]).astype(jnp.int32)
    csum = jnp.cumsum(onehot, axis=0)
    rank = jnp.sum(csum * onehot, axis=1) - 1
    tiles_per = (csum[-1] + tile - 1) // tile
    tile_end = jnp.cumsum(tiles_per)
    tile_start = tile_end - tiles_per
    pos = jnp.sum(onehot * tile_start[None, :], axis=1) * tile + rank
    n_tiles = (TOP_K * m) // tile + n_exp
    tidx = jnp.arange(n_tiles, dtype=jnp.int32)
    n_used = tile_end[-1].astype(jnp.int32)
    texp = jnp.sum((tidx[:, None] >= tile_end[None, :]).astype(jnp.int32), axis=1)
    last = jnp.sum((n_used - 1 >= tile_end).astype(jnp.int32))
    texp = jnp.where(tidx < n_used, texp, last).astype(jnp.int32)
    token = jnp.arange(TOP_K * m, dtype=jnp.int32) // TOP_K
    slot_token = jnp.zeros((n_tiles * tile,), jnp.int32).at[pos].set(token, unique_indices=True)
    pos = pos.reshape(m, TOP_K).astype(jnp.int32)
    return texp, n_used.reshape(1), slot_token.reshape(n_tiles, 1, tile), pos[:, 0], pos[:, 1]


def _rope_tables(seq, ctx_len):
    rows = seq // GRID_W
    row = jnp.repeat(jnp.arange(rows, dtype=jnp.int32), GRID_W).astype(F32)
    col = jnp.tile(jnp.arange(GRID_W, dtype=jnp.int32), rows).astype(F32)
    half = QK_ROPE // 2
    freqs = ROPE_BASE ** (-jnp.arange(0, half, 2, dtype=F32) / half)
    ar, ac = row[:, None] * freqs, col[:, None] * freqs
    cos = jnp.concatenate([jnp.cos(ar), jnp.cos(ar), jnp.cos(ac), jnp.cos(ac)], axis=1)
    sin = jnp.concatenate([-jnp.sin(ar), jnp.sin(ar), -jnp.sin(ac), jnp.sin(ac)], axis=1)
    ones = jnp.ones((seq, QK_NOPE), F32)
    ta = jnp.concatenate([ones, cos, jnp.ones((seq, HEAD_PAD - QK_HEAD), F32)], axis=1)
    tb = jnp.concatenate([0 * ones, sin, jnp.zeros((seq, HEAD_PAD - QK_HEAD), F32)], axis=1)
    ta = jnp.concatenate([jnp.ones((ctx_len, HEAD_PAD), F32), ta], axis=0)
    tb = jnp.concatenate([jnp.zeros((ctx_len, HEAD_PAD), F32), tb], axis=0)
    return ta, tb


def _rope_partner():
    q = QK_ROPE // 4
    return jnp.array(list(range(q, 2 * q)) + list(range(0, q)) + list(range(3 * q, 4 * q)) + list(range(2 * q, 3 * q)),
                     dtype=jnp.int32)


def _head_gains(g):
    perm = _rope_partner()
    zeros = jnp.zeros((HEAD_PAD - QK_HEAD,), F32)
    g_a = jnp.concatenate([g, zeros])
    g_b = jnp.concatenate([jnp.zeros((QK_NOPE,), F32), g[QK_NOPE:][perm], zeros])
    return jnp.stack([g_a, g_b])


def _pack_wq(w_q_b):
    perm = _rope_partner()
    w = w_q_b.reshape(Q_LORA, MLA_HEADS, QK_HEAD)
    w = jnp.concatenate([w, w[:, :, QK_NOPE:][:, :, perm]], axis=-1)
    return w.reshape(Q_LORA, MLA_HEADS * HEAD_PAD).astype(BF16)


def _pack_wkv(w_kv_b):
    perm = _rope_partner()
    w = w_kv_b.reshape(KV_LORA, MLA_HEADS, QK_NOPE + V_HEAD)
    eye = jnp.eye(QK_ROPE, dtype=F32)
    rope_rows = jnp.concatenate([jnp.zeros((QK_ROPE, QK_NOPE), F32), eye, eye[:, perm]], axis=1)
    k_top = jnp.concatenate([w[:, :, :QK_NOPE], jnp.zeros((KV_LORA, MLA_HEADS, HEAD_PAD - QK_NOPE), F32)], axis=-1)
    k_cols = jnp.concatenate([k_top, jnp.broadcast_to(rope_rows[:, None, :], (QK_ROPE, MLA_HEADS, HEAD_PAD))], axis=0)
    k_cols = k_cols.reshape(KV_LORA + QK_ROPE, MLA_HEADS * HEAD_PAD)
    v_cols = jnp.concatenate([w[:, :, QK_NOPE:].reshape(KV_LORA, MLA_HEADS * V_HEAD),
                              jnp.zeros((QK_ROPE, MLA_HEADS * V_HEAD), F32)], axis=0)
    full = jnp.concatenate([k_cols, v_cols], axis=1)
    return jnp.pad(full, ((0, KV_IN_PAD - KV_LORA - QK_ROPE), (0, 0))).astype(BF16)


def _pack_w_in(w_in):
    o1 = 2 * CONV_CH
    o2 = o1 + Q_LORA
    o3 = o2 + KV_LORA + QK_ROPE
    kv = jnp.pad(w_in[:, o2:o3], ((0, 0), (0, KV_IN_PAD - KV_LORA - QK_ROPE)))
    return jnp.concatenate([w_in[:, :o2], kv, w_in[:, o3:]], axis=1).astype(BF16)


def _block_diag(w):
    nd, nblk, bw, _ = w.shape
    eye = jnp.eye(nblk, dtype=w.dtype)
    return jnp.einsum("dgij,gh->dgihj", w, eye).reshape(nd, nblk * bw, nblk * bw).astype(BF16)


def kernel(x, c, ctx, c_ctx, w_ada, b_ada, g_norm1, g_norm2, w_in, b_gate, conv_w, conv_b, conv_ln_g, conv_ln_b, w_o_conv, g_q_a, w_q_b, g_kv_a, w_kv_b, g_qn, g_kn, w_o_mla, rec_conv_w, rec_conv_b, w_ra, b_ra, w_ri, b_ri, lru_lambda, w_o_rec, w_out, w_ff_gate, w_ff_up, w_ff_down, w_router, w_e_gate, w_e_up, w_e_down):
    nb, seq, d = x.shape
    ctx_len = ctx.shape[1]
    depth = w_ada.shape[0]
    t_len = ctx_len + seq
    m = nb * t_len
    n_exp = w_router.shape[-1]
    assert d == 1024 and ctx_len % ROW_TILE == 0 and seq % ROW_TILE == 0 and seq % GRID_W == 0
    assert (TOP_K * m) % MOE_TILE == 0
    dims = dict(nb=nb, t_len=t_len, ctx_len=ctx_len)

    r_pad = -(-(nb + 1) // SUBLANES) * SUBLANES
    cpad = jnp.zeros((r_pad, d), F32).at[:nb].set(c).at[nb].set(c_ctx)
    mods = _ada_call(cpad, w_ada, b_ada).reshape(depth, r_pad, 6, d)

    ta, tb = _rope_tables(seq, ctx_len)
    xs = jnp.concatenate([ctx, x], axis=1).reshape(m, d)

    for i in range(depth):
        moe = i % 2 == 1
        j = i // 2
        ml, mc = mods[i, :nb], mods[i, nb:nb + 1]
        zc, zq, zkv, zr, zg = _inproj_call(xs, ml, mc, g_norm1[i][None], _pack_w_in(w_in[i]), **dims)
        ca = _conv_call(zc, conv_w[i], conv_b[i][None], conv_ln_g[i][None], conv_ln_b[i][None], **dims)
        rr = _rec_call(zr, rec_conv_w[i], rec_conv_b[i][:, None], _block_diag(w_ra[i]), b_ra[i][:, None],
                       _block_diag(w_ri[i]), b_ri[i][:, None], lru_lambda[i][:, None], **dims)
        gkva = jnp.concatenate([g_kv_a[i], jnp.ones((KV_IN_PAD - KV_LORA,), F32)])[None]
        q, k, v = _qkv_call(zq, zkv, ta, tb, g_q_a[i][None], gkva, _pack_wq(w_q_b[i]), _pack_wkv(w_kv_b[i]),
                            _head_gains(g_qn[i]), _head_gains(g_kn[i]), nb=nb, t_len=t_len)
        ao = _attn_call(q, k, v, **dims)
        wrt = w_router[j] if moe else None
        outs = _merge_call(ca, ao, rr, zg, xs, ml, mc, b_gate[i][None], g_norm2[i][None], w_o_conv[i].astype(BF16),
                           w_o_mla[i].astype(BF16), w_o_rec[i].astype(BF16), w_out[i].astype(BF16), wrt, **dims)
        if not moe:
            xs, h2 = outs
            xs = _ffn_call(h2, xs, ml, mc, w_ff_gate[j].astype(BF16), w_ff_up[j].astype(BF16),
                           w_ff_down[j].astype(BF16), **dims)
        else:
            xs, h_tiles, eidx, ew = outs
            texp, n_used, slot_token, pos1, pos2 = _route_tables(eidx, n_exp, MOE_TILE)
            y_tiles = _moe_call(texp, n_used, slot_token, h_tiles, w_e_gate[j].astype(BF16), w_e_up[j].astype(BF16),
                                w_e_down[j].astype(BF16))
            nrt = m // ROW_TILE
            xs = _combine_call(pos1.reshape(nrt, 1, ROW_TILE), pos2.reshape(nrt, 1, ROW_TILE), y_tiles, ew, xs, ml, mc,
                               **dims)
    return xs.reshape(nb, t_len, d)[:, ctx_len:]
```

```python
import functools
import math

import jax
import jax.numpy as jnp
from jax import lax
from jax.experimental import pallas as pl
from jax.experimental.pallas import tpu as pltpu

F32 = jnp.float32
BF16 = jnp.bfloat16

NORM_EPS = 1e-6
GRID_W = 64
CONV_CH = 512
CONV_WIDTH = 31
MLA_HEADS = 8
QK_NOPE = 64
QK_ROPE = 32
V_HEAD = 64
Q_LORA = 768
KV_LORA = 256
QK_HEAD = QK_NOPE + QK_ROPE
REC_WIDTH = 512
REC_CONV = 4
LRU_C = 8.0
N_BRANCH = 3
TOP_K = 2
ROPE_BASE = 10000.0

LANES = 128
SUBLANES = 8
HEAD_PAD = LANES
KV_IN_PAD = 384
ROW_TILE = 256
FFN_CHUNK = 512
FFN_ROWS = 768
ADA_COLS = 1536
MOE_TILE = 512
VMEM_LIMIT = 56 * 1024 * 1024


def _cparams(*sem):
    return pltpu.CompilerParams(dimension_semantics=sem, vmem_limit_bytes=VMEM_LIMIT)


def _const_spec(shape):
    nd = len(shape)
    return pl.BlockSpec(shape, lambda *_: (0,) * nd)


def _sigmoid(x):
    return 0.5 * jnp.tanh(0.5 * x) + 0.5


def _silu(x):
    hx = 0.5 * x
    return hx * jnp.tanh(hx) + hx


def _mod_row(ml_ref, mc_ref, k, is_ctx):
    return jnp.where(is_ctx, mc_ref[0, k:k + 1, :], ml_ref[0, k:k + 1, :])


def _is_ctx(t, tm, ctx_len):
    row = t * tm + lax.broadcasted_iota(jnp.int32, (tm, 1), 0)
    return row < ctx_len


def _rms_mod(x, g, shift, scale):
    y = x * lax.rsqrt(jnp.mean(x * x, axis=-1, keepdims=True) + NORM_EPS) * g
    return y * (1.0 + scale) + shift


def _ada_body(c_ref, w_ref, b_ref, o_ref):
    c = c_ref[...]
    s = _silu(c).astype(BF16)
    o_ref[0] = jnp.dot(s, w_ref[0].astype(BF16), preferred_element_type=F32) + b_ref[0]


def _ada_call(cpad, w_ada, b_ada):
    depth, d, n = w_ada.shape
    r = cpad.shape[0]
    tn = ADA_COLS
    return pl.pallas_call(
        _ada_body,
        grid=(depth, n // tn),
        in_specs=[
            pl.BlockSpec((r, d), lambda l, j: (0, 0)),
            pl.BlockSpec((1, d, tn), lambda l, j: (l, 0, j)),
            pl.BlockSpec((1, 1, tn), lambda l, j: (l, 0, j)),
        ],
        out_specs=pl.BlockSpec((1, r, tn), lambda l, j: (l, 0, j)),
        out_shape=jax.ShapeDtypeStruct((depth, r, n), F32),
        compiler_params=_cparams("arbitrary", "arbitrary"),
        name="ada",
    )(cpad, w_ada, b_ada.reshape(depth, 1, n))


_SEG_WIDTHS = (2 * CONV_CH, Q_LORA, KV_IN_PAD, 2 * REC_WIDTH, N_BRANCH * 1024)
_DOT_COLS = 512


def _inproj_body(x_ref, ml_ref, mc_ref, g_ref, w_ref, *out_refs, tm, ctx_len):
    is_ctx = _is_ctx(pl.program_id(1), tm, ctx_len)
    h = _rms_mod(x_ref[...], g_ref[...], _mod_row(ml_ref, mc_ref, 0, is_ctx), _mod_row(ml_ref, mc_ref, 1, is_ctx))
    hb = h.astype(BF16)
    c0 = 0
    for ref, width in zip(out_refs, _SEG_WIDTHS):
        for j in range(0, width, _DOT_COLS):
            cw = min(_DOT_COLS, width - j)
            ref[:, j:j + cw] = jnp.dot(hb, w_ref[:, c0 + j:c0 + j + cw], preferred_element_type=F32).astype(BF16)
        c0 += width


def _inproj_call(x, ml, mc, g, w, *, nb, t_len, ctx_len):
    m, d = x.shape
    tm = ROW_TILE
    nt = t_len // tm
    row = lambda b, t: (b * nt + t, 0)
    return pl.pallas_call(
        functools.partial(_inproj_body, tm=tm, ctx_len=ctx_len),
        grid=(nb, nt),
        in_specs=[
            pl.BlockSpec((tm, d), row),
            pl.BlockSpec((1, 6, d), lambda b, t: (b, 0, 0)),
            _const_spec((1, 6, d)),
            _const_spec((1, d)),
            pl.BlockSpec(w.shape, lambda b, t: (0, 0), pipeline_mode=pl.Buffered(1)),
        ],
        out_specs=[pl.BlockSpec((tm, wd), row) for wd in _SEG_WIDTHS],
        out_shape=[jax.ShapeDtypeStruct((m, wd), BF16) for wd in _SEG_WIDTHS],
        compiler_params=_cparams("parallel", "parallel"),
        name="inproj",
    )(x, ml, mc, g, w)


_CONV_PAD = 16
_CONV_ROWS = 64


def _conv_body(z_ref, w_ref, b_ref, g_ref, bb_ref, o_ref, u_ref, c_ref, *, t_len, ctx_len):
    ch, pad, rc = CONV_CH, _CONV_PAD, _CONV_ROWS
    zeros = jnp.zeros((pad, ch), F32)
    u_ref[0:pad] = zeros
    u_ref[pad + ctx_len:2 * pad + ctx_len] = zeros
    u_ref[2 * pad + t_len:3 * pad + t_len] = zeros

    def u_row(r0):
        return pl.multiple_of(r0 + pad + jnp.where(r0 >= ctx_len, pad, 0), SUBLANES)

    def glu(i, carry):
        r0 = pl.multiple_of(i * rc, rc)
        z = z_ref[pl.ds(r0, rc), :].astype(F32)
        u_ref[pl.ds(u_row(r0), rc), :] = z[:, :ch] * _sigmoid(z[:, ch:])
        return carry

    lax.fori_loop(0, t_len // rc, glu, 0)

    win = rc + 2 * pad

    def chunk(i, carry):
        r0 = pl.multiple_of(i * rc, rc)
        base = pl.multiple_of(u_row(r0) - pad, SUBLANES)
        for cb in range(ch // LANES):
            ls = slice(cb * LANES, (cb + 1) * LANES)
            w = u_ref[pl.ds(base, win), ls]
            acc = jnp.zeros((rc, LANES), F32)
            for b in range(SUBLANES):
                wb = w if b == 0 else pltpu.roll(w, win - b, axis=0)
                for a in range(win // SUBLANES):
                    k = SUBLANES * a + b - (pad - CONV_WIDTH // 2)
                    if 0 <= k < CONV_WIDTH:
                        acc = acc + w_ref[k:k + 1, ls] * wb[SUBLANES * a:SUBLANES * a + rc]
            c_ref[:, ls] = acc + b_ref[:, ls]
        v = c_ref[...]
        mu = jnp.mean(v, axis=-1, keepdims=True)
        vc = v - mu
        var = jnp.mean(vc * vc, axis=-1, keepdims=True)
        y = vc * lax.rsqrt(var + NORM_EPS) * g_ref[...] + bb_ref[...]
        o_ref[pl.ds(r0, rc), :] = _silu(y).astype(BF16)
        return carry

    lax.fori_loop(0, t_len // rc, chunk, 0)


def _conv_call(zc, w, b, g, bb, *, nb, t_len, ctx_len):
    m = zc.shape[0]
    ch = CONV_CH
    return pl.pallas_call(
        functools.partial(_conv_body, t_len=t_len, ctx_len=ctx_len),
        grid=(nb,),
        in_specs=[
            pl.BlockSpec((t_len, 2 * ch), lambda i: (i, 0)),
            _const_spec((CONV_WIDTH, ch)),
            _const_spec((1, ch)),
            _const_spec((1, ch)),
            _const_spec((1, ch)),
        ],
        out_specs=pl.BlockSpec((t_len, ch), lambda i: (i, 0)),
        out_shape=jax.ShapeDtypeStruct((m, ch), BF16),
        scratch_shapes=[
            pltpu.VMEM((t_len + 3 * _CONV_PAD, ch), F32),
            pltpu.VMEM((_CONV_ROWS, ch), F32),
        ],
        compiler_params=_cparams("parallel"),
        name="conv",
    )(zc, w, b, g, bb)


_REC_ROWS = 128
_REC_PAD = SUBLANES


def _rec_body(z_ref, cw_ref, cb_ref, wa_ref, ba_ref, wi_ref, bi_ref, lam_ref, o_ref, xp_ref, hf_ref, win_ref, *, t_len,
              ctx_len):
    wd, ch, pad = REC_WIDTH, _REC_ROWS, _REC_PAD
    nch = t_len // ch
    nc_ctx = ctx_len // ch
    nblk = ch // SUBLANES
    zeros = jnp.zeros((pad, wd), F32)
    xp_ref[0:pad] = zeros
    xp_ref[pad + ctx_len:2 * pad + ctx_len] = zeros
    xp_ref[2 * pad + t_len:3 * pad + t_len] = zeros

    def xp_row(i):
        return pl.multiple_of(i * ch + pad + jnp.where(i >= nc_ctx, pad, 0), SUBLANES)

    def fill(i, carry):
        r0 = pl.multiple_of(i * ch, ch)
        xp_ref[pl.ds(xp_row(i), ch), :] = z_ref[pl.ds(r0, ch), 0:wd].astype(F32)
        return carry

    lax.fori_loop(0, nch, fill, 0)

    row8 = lax.broadcasted_iota(jnp.int32, (1, SUBLANES, 1), 1)

    def conv4(i, d):
        start = xp_row(i) - (pad if d == 0 else 0)
        win_ref[...] = xp_ref[pl.ds(pl.multiple_of(start, SUBLANES), ch + pad), :]
        lead = pad - (REC_CONV - 1) if d == 0 else 0
        acc = jnp.zeros((ch, wd), F32) + cb_ref[d]
        taps = cw_ref[d]
        for k in range(REC_CONV):
            acc = acc + taps[k:k + 1, :] * win_ref[lead + k:lead + k + ch, :]
        return acc

    def gates(xc, d):
        xb = xc.astype(BF16)
        r = _sigmoid(jnp.dot(xb, wa_ref[d], preferred_element_type=F32) + ba_ref[d])
        ig = _sigmoid(jnp.dot(xb, wi_ref[d], preferred_element_type=F32) + bi_ref[d])
        lam = lam_ref[d]
        softplus_neg = jnp.maximum(-lam, 0.0) + jnp.log1p(jnp.exp(-jnp.abs(lam)))
        log_a = -LRU_C * r * softplus_neg
        a = jnp.exp(log_a)
        bx = jnp.sqrt(-jnp.tanh(log_a) * (1.0 + a * a)) * (ig * xc)
        return a, bx

    def scan_chunk(a, b, carry, d):
        a = a.reshape(nblk, SUBLANES, wd)
        b = b.reshape(nblk, SUBLANES, wd)
        for s in (1, 2, 4):
            shift, keep = (s, row8 >= s) if d == 0 else (SUBLANES - s, row8 < SUBLANES - s)
            a_s, b_s = pltpu.roll(a, shift, axis=1), pltpu.roll(b, shift, axis=1)
            b = jnp.where(keep, a * b_s + b, b)
            a = jnp.where(keep, a * a_s, a)
        outs = [None] * nblk
        for j in (range(nblk) if d == 0 else reversed(range(nblk))):
            hj = b[j] + a[j] * carry
            outs[j] = hj
            carry = hj[SUBLANES - 1:SUBLANES] if d == 0 else hj[0:1]
        return jnp.concatenate(outs, axis=0), carry

    def fwd(i, carry):
        a, bx = gates(conv4(i, 0), 0)
        h, carry = scan_chunk(a, bx, carry, 0)
        hf_ref[pl.ds(pl.multiple_of(i * ch, ch), ch), :] = h
        return carry

    lax.fori_loop(0, nch, fwd, jnp.zeros((1, wd), F32))

    def bwd(j, carry):
        i = jnp.where(j < nc_ctx, nc_ctx - 1 - j, nch - 1 - (j - nc_ctx))
        a, bx = gates(conv4(i, 1), 1)
        h, carry = scan_chunk(a, bx, carry, 1)
        r0 = pl.multiple_of(i * ch, ch)
        gate = z_ref[pl.ds(r0, ch), wd:2 * wd].astype(F32)
        hsum = hf_ref[pl.ds(r0, ch), :] + h
        o_ref[pl.ds(r0, ch), :] = (hsum * jax.nn.gelu(gate)).astype(BF16)
        return carry

    lax.fori_loop(0, nch, bwd, jnp.zeros((1, wd), F32))


def _rec_call(zr, cw, cb, wa, ba, wi, bi, lam, *, nb, t_len, ctx_len):
    m = zr.shape[0]
    wd = REC_WIDTH
    return pl.pallas_call(
        functools.partial(_rec_body, t_len=t_len, ctx_len=ctx_len),
        grid=(nb,),
        in_specs=[
            pl.BlockSpec((t_len, 2 * wd), lambda i: (i, 0)),
            _const_spec((2, REC_CONV, wd)),
            _const_spec((2, 1, wd)),
            _const_spec((2, wd, wd)),
            _const_spec((2, 1, wd)),
            _const_spec((2, wd, wd)),
            _const_spec((2, 1, wd)),
            _const_spec((2, 1, wd)),
        ],
        out_specs=pl.BlockSpec((t_len, wd), lambda i: (i, 0)),
        out_shape=jax.ShapeDtypeStruct((m, wd), BF16),
        scratch_shapes=[
            pltpu.VMEM((t_len + 3 * _REC_PAD, wd), F32),
            pltpu.VMEM((t_len, wd), F32),
            pltpu.VMEM((_REC_ROWS + _REC_PAD, wd), F32),
        ],
        compiler_params=_cparams("parallel"),
        name="rglru",
    )(zr, cw, cb, wa, ba, wi, bi, lam)


def _qkv_body(zq_ref, zkv_ref, ta_ref, tb_ref, gqa_ref, gkva_ref, wq_ref, wkv_ref, gq_ref, gk_ref, q_ref, k_ref, v_ref):
    hp = HEAD_PAD
    zq = zq_ref[...].astype(F32)
    qa = zq * lax.rsqrt(jnp.mean(zq * zq, axis=-1, keepdims=True) + NORM_EPS) * gqa_ref[...]
    q = jnp.dot(qa.astype(BF16), wq_ref[...], preferred_element_type=F32)

    zkv = zkv_ref[...].astype(F32)
    lora = lax.broadcasted_iota(jnp.int32, (1, KV_IN_PAD), 1) < KV_LORA
    ms = jnp.sum(jnp.where(lora, zkv * zkv, 0.0), axis=-1, keepdims=True) * (1.0 / KV_LORA)
    lhs = jnp.where(lora, zkv * lax.rsqrt(ms + NORM_EPS) * gkva_ref[...], zkv)
    kv = jnp.dot(lhs.astype(BF16), wkv_ref[...], preferred_element_type=F32)
    nh = MLA_HEADS * hp
    lane_v = lax.broadcasted_iota(jnp.int32, (1, nh), 1) & (hp - 1)
    v_ref[...] = (kv[:, 2 * nh:] + jnp.where(lane_v == V_HEAD, 1.0, 0.0)).astype(BF16)

    ta, tb = ta_ref[...], tb_ref[...]
    scale = math.log2(math.e) / math.sqrt(QK_HEAD)
    q_a, q_b = ta * gq_ref[0:1, :] * scale, tb * gq_ref[1:2, :] * scale
    k_a, k_b = ta * gk_ref[0:1, :], tb * gk_ref[1:2, :]
    for h in range(MLA_HEADS):
        sl = slice(h * hp, (h + 1) * hp)
        sp = slice(nh + h * hp, nh + (h + 1) * hp)
        for src, fa, fb, ref in ((q, q_a, q_b, q_ref), (kv, k_a, k_b, k_ref)):
            t, tp = src[:, sl], src[:, sp]
            rs = lax.rsqrt(jnp.sum(t * t, axis=-1, keepdims=True) * (1.0 / QK_HEAD) + NORM_EPS)
            ref[:, sl] = (rs * (t * fa + tp * fb)).astype(BF16)


def _qkv_call(zq, zkv, ta, tb, gqa, gkva, wq, wkv, gq, gk, *, nb, t_len):
    m = zq.shape[0]
    tm = ROW_TILE
    nt = t_len // tm
    row = lambda b, t: (b * nt + t, 0)
    pos = lambda b, t: (t, 0)
    nq = MLA_HEADS * HEAD_PAD
    nv = nq
    return pl.pallas_call(
        _qkv_body,
        grid=(nb, nt),
        in_specs=[
            pl.BlockSpec((tm, Q_LORA), row),
            pl.BlockSpec((tm, KV_IN_PAD), row),
            pl.BlockSpec((tm, HEAD_PAD), pos),
            pl.BlockSpec((tm, HEAD_PAD), pos),
            _const_spec((1, Q_LORA)),
            _const_spec((1, KV_IN_PAD)),
            _const_spec(wq.shape),
            _const_spec(wkv.shape),
            _const_spec((2, HEAD_PAD)),
            _const_spec((2, HEAD_PAD)),
        ],
        out_specs=[pl.BlockSpec((tm, nq), row), pl.BlockSpec((tm, nq), row), pl.BlockSpec((tm, nv), row)],
        out_shape=[jax.ShapeDtypeStruct((m, nq), BF16), jax.ShapeDtypeStruct((m, nq), BF16),
                   jax.ShapeDtypeStruct((m, nv), BF16)],
        compiler_params=_cparams("parallel", "parallel"),
        name="qkv",
    )(zq, zkv, ta, tb, gqa, gkva, wq, wkv, gq, gk)


def _attn_body(q_ref, k_ref, v_ref, o_ref, *, t_len, ctx_len):
    hp = HEAD_PAD
    low = lax.broadcasted_iota(jnp.int32, (1, hp), 1) < V_HEAD

    def attend(nk):
        def scores(h):
            sl = slice(h * hp, (h + 1) * hp)
            return lax.dot_general(q_ref[:, sl], k_ref[0:nk, sl], (((1,), (1,)), ((), ())),
                                   preferred_element_type=F32)

        s = scores(0)
        outs = []
        for h in range(MLA_HEADS):
            s_next = scores(h + 1) if h + 1 < MLA_HEADS else None
            p = jnp.exp2(s - jnp.max(s, axis=-1, keepdims=True)).astype(BF16)
            pv = jnp.dot(p, v_ref[0:nk, h * hp:(h + 1) * hp], preferred_element_type=F32)
            outs.append(pv / pv[:, V_HEAD:V_HEAD + 1])
            if h % 2 == 1:
                both = jnp.where(low, outs[h - 1], pltpu.roll(outs[h], V_HEAD, axis=1))
                o_ref[:, (h // 2) * hp:(h // 2 + 1) * hp] = both.astype(BF16)
            s = s_next

    is_ctx_tile = pl.program_id(1) * ROW_TILE < ctx_len

    @pl.when(is_ctx_tile)
    def _():
        attend(ctx_len)

    @pl.when(jnp.logical_not(is_ctx_tile))
    def _():
        attend(t_len)


def _attn_call(q, k, v, *, nb, t_len, ctx_len):
    m = q.shape[0]
    tq = ROW_TILE
    nt = t_len // tq
    nq = MLA_HEADS * HEAD_PAD
    nv = MLA_HEADS * V_HEAD
    return pl.pallas_call(
        functools.partial(_attn_body, t_len=t_len, ctx_len=ctx_len),
        grid=(nb, nt),
        in_specs=[
            pl.BlockSpec((tq, nq), lambda b, t: (b * nt + t, 0)),
            pl.BlockSpec((t_len, nq), lambda b, t: (b, 0)),
            pl.BlockSpec((t_len, nq), lambda b, t: (b, 0)),
        ],
        out_specs=pl.BlockSpec((tq, nv), lambda b, t: (b * nt + t, 0)),
        out_shape=jax.ShapeDtypeStruct((m, nv), BF16),
        compiler_params=_cparams("parallel", "arbitrary"),
        name="attn",
    )(q, k, v)


_ROUTER_LOW_LANE = 64


def _merge_body(ca_ref, ao_ref, rr_ref, zg_ref, x_ref, ml_ref, mc_ref, bg_ref, g2_ref, wc_ref, wm_ref, wr_ref,
                wo_ref, *rest, tm, ctx_len, n_exp):
    moe = n_exp > 0
    d = x_ref.shape[-1]
    is_ctx = _is_ctx(pl.program_id(1), tm, ctx_len)
    merged = jnp.zeros((tm, d), F32)
    for j, (src, w) in enumerate(((ca_ref, wc_ref), (ao_ref, wm_ref), (rr_ref, wr_ref))):
        gate = _sigmoid(zg_ref[:, j * d:(j + 1) * d].astype(F32) + bg_ref[:, j * d:(j + 1) * d])
        merged = merged + gate * jnp.dot(src[...], w[...], preferred_element_type=F32)
    y = jnp.dot(merged.astype(BF16), wo_ref[...], preferred_element_type=F32)
    x = x_ref[...] + _mod_row(ml_ref, mc_ref, 2, is_ctx) * y
    h2 = _rms_mod(x, g2_ref[...], _mod_row(ml_ref, mc_ref, 3, is_ctx), _mod_row(ml_ref, mc_ref, 4, is_ctx))
    if not moe:
        xo_ref, h_ref = rest
        xo_ref[...] = x
        h_ref[...] = h2.astype(BF16)
        return
    wrt_ref, xo_ref, h_ref, ei_ref, ew_ref = rest
    xo_ref[...] = x
    for s in range(d // LANES):
        h_ref[pl.ds(s, tm, stride=SUBLANES), :] = h2[:, s * LANES:(s + 1) * LANES]
    h_hi = h2.astype(BF16)
    h_lo = (h2 - h_hi.astype(F32)).astype(BF16)
    part = (jnp.dot(h_hi, wrt_ref[...], preferred_element_type=F32)
            + jnp.dot(h_lo, wrt_ref[...], preferred_element_type=F32))
    logits = part + pltpu.roll(part, LANES - _ROUTER_LOW_LANE, axis=1)
    lane = lax.broadcasted_iota(jnp.int32, (tm, LANES), 1).astype(F32)
    logits = jnp.where(lane < n_exp, logits, -jnp.inf)
    m1 = jnp.max(logits, axis=-1, keepdims=True)
    i1 = jnp.min(jnp.where(logits == m1, lane, float(LANES)), axis=-1, keepdims=True)
    rest_l = jnp.where(lane == i1, -jnp.inf, logits)
    m2 = jnp.max(rest_l, axis=-1, keepdims=True)
    i2 = jnp.min(jnp.where(rest_l == m2, lane, float(LANES)), axis=-1, keepdims=True)
    e2 = jnp.exp(m2 - m1)
    w1 = 1.0 / (1.0 + e2)
    w2 = e2 / (1.0 + e2)
    ei_ref[...] = jnp.where(lane == 0.0, i1, jnp.where(lane == 1.0, i2, 0.0)).astype(jnp.int32)
    ew_ref[...] = jnp.where(lane == 0.0, w1, jnp.where(lane == 1.0, w2, 0.0))


def _merge_call(ca, ao, rr, zg, x, ml, mc, bg, g2, wc, wm, wr, wo, w_router, *, nb, t_len, ctx_len):
    m, d = x.shape
    tm = ROW_TILE
    nt = t_len // tm
    row = lambda b, t: (b * nt + t, 0)
    moe = w_router is not None
    n_exp = w_router.shape[1] if moe else 0
    wrt = None
    if moe:
        assert n_exp <= _ROUTER_LOW_LANE
        w_hi = w_router.astype(BF16)
        w_lo = (w_router - w_hi.astype(F32)).astype(BF16)
        wrt = jnp.zeros((d, LANES), BF16).at[:, :n_exp].set(w_hi).at[:, _ROUTER_LOW_LANE:_ROUTER_LOW_LANE + n_exp].set(w_lo)
    in_specs = [
        pl.BlockSpec((tm, ca.shape[1]), row),
        pl.BlockSpec((tm, ao.shape[1]), row),
        pl.BlockSpec((tm, rr.shape[1]), row),
        pl.BlockSpec((tm, zg.shape[1]), row),
        pl.BlockSpec((tm, d), row),
        pl.BlockSpec((1, 6, d), lambda b, t: (b, 0, 0)),
        _const_spec((1, 6, d)),
        _const_spec(bg.shape),
        _const_spec(g2.shape),
        _const_spec(wc.shape),
        _const_spec(wm.shape),
        _const_spec(wr.shape),
        _const_spec(wo.shape),
    ]
    args = [ca, ao, rr, zg, x, ml, mc, bg, g2, wc, wm, wr, wo]
    out_specs = [pl.BlockSpec((tm, d), row)]
    out_shape = [jax.ShapeDtypeStruct((m, d), F32)]
    if moe:
        in_specs.append(_const_spec(wrt.shape))
        args.append(wrt)
        out_specs += [pl.BlockSpec((tm * SUBLANES, LANES), row), pl.BlockSpec((tm, LANES), row),
                      pl.BlockSpec((tm, LANES), row)]
        out_shape += [jax.ShapeDtypeStruct((m * SUBLANES, LANES), F32), jax.ShapeDtypeStruct((m, LANES), jnp.int32),
                      jax.ShapeDtypeStruct((m, LANES), F32)]
    else:
        out_specs.append(pl.BlockSpec((tm, d), row))
        out_shape.append(jax.ShapeDtypeStruct((m, d), BF16))
    return pl.pallas_call(
        functools.partial(_merge_body, tm=tm, ctx_len=ctx_len, n_exp=n_exp),
        grid=(nb, nt),
        in_specs=in_specs,
        out_specs=out_specs,
        out_shape=out_shape,
        input_output_aliases={4: 0},
        compiler_params=_cparams("parallel", "parallel"),
        name="merge_moe" if moe else "merge",
    )(*args)


def _ffn_body(h_ref, x_ref, ml_ref, mc_ref, wg_ref, wu_ref, wd_ref, o_ref, acc_ref, *, tm, ctx_len):
    f = pl.program_id(2)

    @pl.when(f == 0)
    def _():
        acc_ref[...] = jnp.zeros_like(acc_ref)

    h = h_ref[...]
    g = jnp.dot(h, wg_ref[0], preferred_element_type=F32)
    u = jnp.dot(h, wu_ref[0], preferred_element_type=F32)
    acc_ref[...] += jnp.dot((_silu(g) * u).astype(BF16), wd_ref[...], preferred_element_type=F32)

    @pl.when(f == pl.num_programs(2) - 1)
    def _():
        is_ctx = _is_ctx(pl.program_id(1), tm, ctx_len)
        o_ref[...] = x_ref[...] + _mod_row(ml_ref, mc_ref, 5, is_ctx) * acc_ref[...]


def _ffn_call(h, x, ml, mc, wg, wu, wd, *, nb, t_len, ctx_len):
    m, d = x.shape
    nf, _, tf = wg.shape
    tm = FFN_ROWS if t_len % FFN_ROWS == 0 else ROW_TILE
    nt = t_len // tm
    row = lambda b, t, f: (b * nt + t, 0)
    return pl.pallas_call(
        functools.partial(_ffn_body, tm=tm, ctx_len=ctx_len),
        grid=(nb, nt, nf),
        in_specs=[
            pl.BlockSpec((tm, d), row),
            pl.BlockSpec((tm, d), row),
            pl.BlockSpec((1, 6, d), lambda b, t, f: (b, 0, 0)),
            pl.BlockSpec((1, 6, d), lambda b, t, f: (0, 0, 0)),
            pl.BlockSpec((1, d, tf), lambda b, t, f: (f, 0, 0)),
            pl.BlockSpec((1, d, tf), lambda b, t, f: (f, 0, 0)),
            pl.BlockSpec((tf, d), lambda b, t, f: (f, 0)),
        ],
        out_specs=pl.BlockSpec((tm, d), row),
        out_shape=jax.ShapeDtypeStruct((m, d), F32),
        scratch_shapes=[pltpu.VMEM((tm, d), F32)],
        input_output_aliases={1: 0},
        compiler_params=_cparams("parallel", "parallel", "arbitrary"),
        name="ffn",
    )(h, x, ml, mc, wg, wu, wd)


def _gather_tiles(idx_ref, src_hbm, dst_ref, slot, sem, n):
    def issue(r, carry):
        src = pl.multiple_of(idx_ref[0, 0, r] * SUBLANES, SUBLANES)
        pltpu.make_async_copy(src_hbm.at[pl.ds(src, SUBLANES), :],
                              dst_ref.at[slot, pl.ds(pl.multiple_of(r * SUBLANES, SUBLANES), SUBLANES), :],
                              sem.at[slot]).start()
        return carry

    lax.fori_loop(0, n, issue, 0, unroll=8)


def _wait_tiles(src_hbm, dst_ref, slot, sem, n):
    pltpu.make_async_copy(src_hbm.at[pl.ds(0, n * SUBLANES), :], dst_ref.at[slot], sem.at[slot]).wait()


def _moe_body(te_ref, nu_ref, st_ref, stn_ref, h_hbm, wg_ref, wu_ref, wd_ref, y_ref, xt_ref, xb_ref, acc_ref, sem, *,
              tm):
    i, f = pl.program_id(0), pl.program_id(1)
    nf = pl.num_programs(1)
    d = xb_ref.shape[1]
    n_used = nu_ref[0]
    valid = i < n_used
    slot = i & 1

    @pl.when(jnp.logical_and(i == 0, f == 0))
    def _():
        _gather_tiles(st_ref, h_hbm, xt_ref, 0, sem, tm)

    @pl.when(jnp.logical_and(valid, f == 0))
    def _():
        _wait_tiles(h_hbm, xt_ref, slot, sem, tm)
        for s in range(d // LANES):
            xb_ref[:, s * LANES:(s + 1) * LANES] = xt_ref[slot, pl.ds(s, tm, stride=SUBLANES), :].astype(BF16)
        acc_ref[...] = jnp.zeros_like(acc_ref)

    @pl.when(jnp.logical_and(i + 1 < n_used, f == 0))
    def _():
        _gather_tiles(stn_ref, h_hbm, xt_ref, 1 - slot, sem, tm)

    @pl.when(valid)
    def _():
        x = xb_ref[...]
        g = jnp.dot(x, wg_ref[0, 0], preferred_element_type=F32)
        u = jnp.dot(x, wu_ref[0, 0], preferred_element_type=F32)
        acc_ref[...] += jnp.dot((_silu(g) * u).astype(BF16), wd_ref[0], preferred_element_type=F32)

    @pl.when(jnp.logical_and(valid, f == nf - 1))
    def _():
        for s in range(d // LANES):
            y_ref[pl.ds(s, tm, stride=SUBLANES), :] = acc_ref[:, s * LANES:(s + 1) * LANES]

    @pl.when(jnp.logical_and(jnp.logical_not(valid), f == nf - 1))
    def _():
        y_ref[...] = jnp.zeros_like(y_ref)


def _moe_call(tile_expert, n_used, slot_token, h_tiles, wg, wu, wd):
    n_exp, nf, d, tf = wg.shape
    tm = MOE_TILE
    n_tiles = slot_token.shape[0]

    def w_in_map(i, f, te, nu):
        return (te[i], jnp.where(i < nu[0], f, nf - 1), 0, 0)

    def w_out_map(i, f, te, nu):
        return (te[i], jnp.where(i < nu[0], f, nf - 1), 0)

    grid_spec = pltpu.PrefetchScalarGridSpec(
        num_scalar_prefetch=2,
        grid=(n_tiles, nf),
        in_specs=[
            pl.BlockSpec((1, 1, tm), lambda i, f, te, nu: (i, 0, 0), memory_space=pltpu.SMEM),
            pl.BlockSpec((1, 1, tm), lambda i, f, te, nu: (jnp.minimum(i + 1, n_tiles - 1), 0, 0),
                         memory_space=pltpu.SMEM),
            pl.BlockSpec(memory_space=pl.ANY),
            pl.BlockSpec((1, 1, d, tf), w_in_map),
            pl.BlockSpec((1, 1, d, tf), w_in_map),
            pl.BlockSpec((1, tf, d), w_out_map),
        ],
        out_specs=pl.BlockSpec((tm * SUBLANES, LANES), lambda i, f, te, nu: (i, 0)),
        scratch_shapes=[
            pltpu.VMEM((2, tm * SUBLANES, LANES), F32),
            pltpu.VMEM((tm, d), BF16),
            pltpu.VMEM((tm, d), F32),
            pltpu.SemaphoreType.DMA((2,)),
        ],
    )
    return pl.pallas_call(
        functools.partial(_moe_body, tm=tm),
        grid_spec=grid_spec,
        out_shape=jax.ShapeDtypeStruct((n_tiles * tm * SUBLANES, LANES), F32),
        compiler_params=_cparams("arbitrary", "arbitrary"),
        name="moe",
    )(tile_expert, n_used, slot_token, slot_token, h_tiles, wg, wu, wd)


def _combine_body(p1_ref, p2_ref, p1n_ref, p2n_ref, y_hbm, ew_ref, x_ref, ml_ref, mc_ref, o_ref, y1_ref, y2_ref, sem,
                  *, tm, ctx_len, tile_of, nt):
    n = pl.program_id(0)
    d = x_ref.shape[1]
    slot = n & 1

    @pl.when(n == 0)
    def _():
        _gather_tiles(p1_ref, y_hbm, y1_ref, 0, sem.at[0], tm)
        _gather_tiles(p2_ref, y_hbm, y2_ref, 0, sem.at[1], tm)

    @pl.when(n + 1 < pl.num_programs(0))
    def _():
        _gather_tiles(p1n_ref, y_hbm, y1_ref, 1 - slot, sem.at[0], tm)
        _gather_tiles(p2n_ref, y_hbm, y2_ref, 1 - slot, sem.at[1], tm)

    _wait_tiles(y_hbm, y1_ref, slot, sem.at[0], tm)
    _wait_tiles(y_hbm, y2_ref, slot, sem.at[1], tm)
    is_ctx = _is_ctx(tile_of(n) % nt, tm, ctx_len)
    w1, w2 = ew_ref[:, 0:1], ew_ref[:, 1:2]
    ga = _mod_row(ml_ref, mc_ref, 5, is_ctx)
    for s in range(d // LANES):
        ls = slice(s * LANES, (s + 1) * LANES)
        f = (w1 * y1_ref[slot, pl.ds(s, tm, stride=SUBLANES), :]
             + w2 * y2_ref[slot, pl.ds(s, tm, stride=SUBLANES), :])
        o_ref[:, ls] = x_ref[:, ls] + ga[:, ls] * f


def _combine_call(pos1, pos2, y_tiles, ew, x, ml, mc, *, nb, t_len, ctx_len, latent_only):
    m, d = x.shape
    tm = ROW_TILE
    nt = t_len // tm
    nc = ctx_len // tm if latent_only else 0
    per = nt - nc
    n_steps = nb * per

    def tile_of(n):
        return (n // per) * nt + nc + n % per

    cur = lambda n: (n, 0, 0)
    nxt = lambda n: (jnp.minimum(n + 1, n_steps - 1), 0, 0)
    row = lambda n: (tile_of(n), 0)
    return pl.pallas_call(
        functools.partial(_combine_body, tm=tm, ctx_len=ctx_len, tile_of=tile_of, nt=nt),
        grid=(n_steps,),
        in_specs=[
            pl.BlockSpec((1, 1, tm), cur, memory_space=pltpu.SMEM),
            pl.BlockSpec((1, 1, tm), cur, memory_space=pltpu.SMEM),
            pl.BlockSpec((1, 1, tm), nxt, memory_space=pltpu.SMEM),
            pl.BlockSpec((1, 1, tm), nxt, memory_space=pltpu.SMEM),
            pl.BlockSpec(memory_space=pl.ANY),
            pl.BlockSpec((tm, LANES), row),
            pl.BlockSpec((tm, d), row),
            pl.BlockSpec((1, 6, d), lambda n: (n // per, 0, 0)),
            _const_spec((1, 6, d)),
        ],
        out_specs=pl.BlockSpec((tm, d), lambda n: (n, 0)),
        out_shape=jax.ShapeDtypeStruct((n_steps * tm, d), F32),
        scratch_shapes=[
            pltpu.VMEM((2, tm * SUBLANES, LANES), F32),
            pltpu.VMEM((2, tm * SUBLANES, LANES), F32),
            pltpu.SemaphoreType.DMA((2, 2)),
        ],
        input_output_aliases={} if latent_only else {6: 0},
        compiler_params=_cparams("arbitrary"),
        name="moe_combine",
    )(pos1, pos2, pos1, pos2, y_tiles, ew, x, ml, mc)


def _route_tables(eidx, token_rows, n_exp, tile):
    m = eidx.shape[0]
    e = eidx.reshape(-1)
    onehot = (e[:, None] == jnp.arange(n_exp, dtype=jnp.int32)[None, :]).astype(jnp.int32)
    csum = jnp.cumsum(onehot, axis=0)
    rank = jnp.sum(csum * onehot, axis=1) - 1
    tiles_per = (csum[-1] + tile - 1) // tile
    tile_end = jnp.cumsum(tiles_per)
    tile_start = tile_end - tiles_per
    pos = jnp.sum(onehot * tile_start[None, :], axis=1) * tile + rank
    n_tiles = (TOP_K * m) // tile + n_exp
    tidx = jnp.arange(n_tiles, dtype=jnp.int32)
    n_used = tile_end[-1].astype(jnp.int32)
    texp = jnp.sum((tidx[:, None] >= tile_end[None, :]).astype(jnp.int32), axis=1)
    last = jnp.sum((n_used - 1 >= tile_end).astype(jnp.int32))
    texp = jnp.where(tidx < n_used, texp, last).astype(jnp.int32)
    slot_token = jnp.zeros((n_tiles * tile,), jnp.int32).at[pos].set(jnp.repeat(token_rows, TOP_K),
                                                                   unique_indices=True)
    pos = pos.reshape(m, TOP_K).astype(jnp.int32)
    return texp, n_used.reshape(1), slot_token.reshape(n_tiles, 1, tile), pos[:, 0], pos[:, 1]


def _rope_tables(seq, ctx_len):
    rows = seq // GRID_W
    row = jnp.repeat(jnp.arange(rows, dtype=jnp.int32), GRID_W).astype(F32)
    col = jnp.tile(jnp.arange(GRID_W, dtype=jnp.int32), rows).astype(F32)
    half = QK_ROPE // 2
    freqs = ROPE_BASE ** (-jnp.arange(0, half, 2, dtype=F32) / half)
    ar, ac = row[:, None] * freqs, col[:, None] * freqs
    cos = jnp.concatenate([jnp.cos(ar), jnp.cos(ar), jnp.cos(ac), jnp.cos(ac)], axis=1)
    sin = jnp.concatenate([-jnp.sin(ar), jnp.sin(ar), -jnp.sin(ac), jnp.sin(ac)], axis=1)
    ones = jnp.ones((seq, QK_NOPE), F32)
    ta = jnp.concatenate([ones, cos, jnp.ones((seq, HEAD_PAD - QK_HEAD), F32)], axis=1)
    tb = jnp.concatenate([0 * ones, sin, jnp.zeros((seq, HEAD_PAD - QK_HEAD), F32)], axis=1)
    ta = jnp.concatenate([jnp.ones((ctx_len, HEAD_PAD), F32), ta], axis=0)
    tb = jnp.concatenate([jnp.zeros((ctx_len, HEAD_PAD), F32), tb], axis=0)
    return ta, tb


def _rope_partner():
    q = QK_ROPE // 4
    return jnp.array(list(range(q, 2 * q)) + list(range(0, q)) + list(range(3 * q, 4 * q)) + list(range(2 * q, 3 * q)),
                     dtype=jnp.int32)


def _head_gains(g):
    perm = _rope_partner()
    zeros = jnp.zeros((HEAD_PAD - QK_HEAD,), F32)
    g_a = jnp.concatenate([g, zeros])
    g_b = jnp.concatenate([jnp.zeros((QK_NOPE,), F32), g[QK_NOPE:][perm], zeros])
    return jnp.stack([g_a, g_b])


def _pack_wq(w_q_b):
    perm = _rope_partner()
    w = w_q_b.reshape(Q_LORA, MLA_HEADS, QK_HEAD)
    tail = jnp.zeros((Q_LORA, MLA_HEADS, HEAD_PAD - QK_HEAD), F32)
    main = jnp.concatenate([w, tail], axis=-1)
    partner = jnp.concatenate([jnp.zeros((Q_LORA, MLA_HEADS, QK_NOPE), F32), w[:, :, QK_NOPE:][:, :, perm], tail], axis=-1)
    nh = MLA_HEADS * HEAD_PAD
    return jnp.concatenate([main.reshape(Q_LORA, nh), partner.reshape(Q_LORA, nh)], axis=1).astype(BF16)


def _pack_wkv(w_kv_b):
    perm = _rope_partner()
    w = w_kv_b.reshape(KV_LORA, MLA_HEADS, QK_NOPE + V_HEAD)
    eye = jnp.eye(QK_ROPE, dtype=F32)
    left = jnp.zeros((QK_ROPE, QK_NOPE), F32)
    right = jnp.zeros((QK_ROPE, HEAD_PAD - QK_HEAD), F32)
    nh = MLA_HEADS * HEAD_PAD

    def per_head(rope_rows):
        return jnp.broadcast_to(rope_rows[:, None, :], (QK_ROPE, MLA_HEADS, HEAD_PAD)).reshape(QK_ROPE, nh)

    k_top = jnp.concatenate([w[:, :, :QK_NOPE], jnp.zeros((KV_LORA, MLA_HEADS, HEAD_PAD - QK_NOPE), F32)], axis=-1)
    main = jnp.concatenate([k_top.reshape(KV_LORA, nh), per_head(jnp.concatenate([left, eye, right], axis=1))], axis=0)
    partner = jnp.concatenate([jnp.zeros((KV_LORA, nh), F32),
                               per_head(jnp.concatenate([left, eye[:, perm], right], axis=1))], axis=0)
    v_top = jnp.concatenate([w[:, :, QK_NOPE:], jnp.zeros((KV_LORA, MLA_HEADS, HEAD_PAD - V_HEAD), F32)], axis=-1)
    v_cols = jnp.concatenate([v_top.reshape(KV_LORA, nh), jnp.zeros((QK_ROPE, nh), F32)], axis=0)
    full = jnp.concatenate([main, partner, v_cols], axis=1)
    return jnp.pad(full, ((0, KV_IN_PAD - KV_LORA - QK_ROPE), (0, 0))).astype(BF16)


def _pack_w_in(w_in):
    o1 = 2 * CONV_CH
    o2 = o1 + Q_LORA
    o3 = o2 + KV_LORA + QK_ROPE
    kv = jnp.pad(w_in[:, o2:o3], ((0, 0), (0, KV_IN_PAD - KV_LORA - QK_ROPE)))
    return jnp.concatenate([w_in[:, :o2], kv, w_in[:, o3:]], axis=1).astype(BF16)


def _chunk_cols(w):
    *lead, d, dff = w.shape
    w = w.reshape(*lead, d, dff // FFN_CHUNK, FFN_CHUNK)
    return jnp.swapaxes(w, -3, -2).astype(BF16)


def _block_diag(w):
    nd, nblk, bw, _ = w.shape
    eye = jnp.eye(nblk, dtype=w.dtype)
    return jnp.einsum("dgij,gh->dgihj", w, eye).reshape(nd, nblk * bw, nblk * bw).astype(BF16)


def kernel(x, c, ctx, c_ctx, w_ada, b_ada, g_norm1, g_norm2, w_in, b_gate, conv_w, conv_b, conv_ln_g, conv_ln_b, w_o_conv, g_q_a, w_q_b, g_kv_a, w_kv_b, g_qn, g_kn, w_o_mla, rec_conv_w, rec_conv_b, w_ra, b_ra, w_ri, b_ri, lru_lambda, w_o_rec, w_out, w_ff_gate, w_ff_up, w_ff_down, w_router, w_e_gate, w_e_up, w_e_down):
    nb, seq, d = x.shape
    ctx_len = ctx.shape[1]
    depth = w_ada.shape[0]
    t_len = ctx_len + seq
    m = nb * t_len
    n_exp = w_router.shape[-1]
    assert d == 1024 and ctx_len % ROW_TILE == 0 and seq % ROW_TILE == 0 and seq % GRID_W == 0
    assert (TOP_K * m) % MOE_TILE == 0 and (TOP_K * nb * seq) % MOE_TILE == 0
    dims =dict(nb=nb, t_len=t_len, ctx_len=ctx_len)

    r_pad = -(-(nb + 1) // SUBLANES) * SUBLANES
    cpad = jnp.zeros((r_pad, d), F32).at[:nb].set(c).at[nb].set(c_ctx)
    mods = _ada_call(cpad, w_ada, b_ada).reshape(depth, r_pad, 6, d)

    ta, tb = _rope_tables(seq, ctx_len)
    xs = jnp.concatenate([ctx, x], axis=1).reshape(m, d)

    for i in range(depth):
        moe = i % 2 == 1
        j = i // 2
        ml, mc = mods[i, :nb], mods[i, nb:nb + 1]
        zc, zq, zkv, zr, zg = _inproj_call(xs, ml, mc, g_norm1[i][None], _pack_w_in(w_in[i]), **dims)
        ca = _conv_call(zc, conv_w[i], conv_b[i][None], conv_ln_g[i][None], conv_ln_b[i][None], **dims)
        rr = _rec_call(zr, rec_conv_w[i], rec_conv_b[i][:, None], _block_diag(w_ra[i]), b_ra[i][:, None],
                       _block_diag(w_ri[i]), b_ri[i][:, None], lru_lambda[i][:, None], **dims)
        gkva = jnp.concatenate([g_kv_a[i], jnp.ones((KV_IN_PAD - KV_LORA,), F32)])[None]
        q, k, v = _qkv_call(zq, zkv, ta, tb, g_q_a[i][None], gkva, _pack_wq(w_q_b[i]), _pack_wkv(w_kv_b[i]),
                            _head_gains(g_qn[i]), _head_gains(g_kn[i]), nb=nb, t_len=t_len)
        ao = _attn_call(q, k, v, **dims)
        wrt = w_router[j] if moe else None
        outs = _merge_call(ca, ao, rr, zg, xs, ml, mc, b_gate[i][None], g_norm2[i][None], w_o_conv[i].astype(BF16),
                           w_o_mla[i].astype(BF16), w_o_rec[i].astype(BF16), w_out[i].astype(BF16), wrt, **dims)
        if not moe:
            xs, h2 = outs
            xs = _ffn_call(h2, xs, ml, mc, _chunk_cols(w_ff_gate[j]), _chunk_cols(w_ff_up[j]),
                           w_ff_down[j].astype(BF16), **dims)
            continue
        xs, h_tiles, eidx, ew = outs
        latent_only = i == depth - 1
        rows = jnp.arange(m, dtype=jnp.int32).reshape(nb, t_len)
        experts = eidx[:, :TOP_K].reshape(nb, t_len, TOP_K)
        if latent_only:
            rows, experts = rows[:, ctx_len:], experts[:, ctx_len:]
        rows = rows.reshape(-1)
        texp, n_used, slot_token, pos1, pos2 = _route_tables(experts.reshape(-1, TOP_K), rows, n_exp, MOE_TILE)
        y_tiles = _moe_call(texp, n_used, slot_token, h_tiles, _chunk_cols(w_e_gate[j]), _chunk_cols(w_e_up[j]),
                            w_e_down[j].astype(BF16))
        nrt = rows.shape[0] // ROW_TILE
        xs = _combine_call(pos1.reshape(nrt, 1, ROW_TILE), pos2.reshape(nrt, 1, ROW_TILE), y_tiles, ew, xs, ml, mc,
                           latent_only=latent_only, **dims)
        if latent_only:
            return xs.reshape(nb, seq, d)
    return xs.reshape(nb, t_len, d)[:, ctx_len:]
```

```python
import functools
import math

import jax
import jax.numpy as jnp
from jax import lax
from jax.experimental import pallas as pl
from jax.experimental.pallas import tpu as pltpu

F32 = jnp.float32
BF16 = jnp.bfloat16

NORM_EPS = 1e-6
GRID_W = 64
CONV_CH = 512
CONV_WIDTH = 31
MLA_HEADS = 8
QK_NOPE = 64
QK_ROPE = 32
V_HEAD = 64
Q_LORA = 768
KV_LORA = 256
QK_HEAD = QK_NOPE + QK_ROPE
REC_WIDTH = 512
REC_CONV = 4
LRU_C = 8.0
N_BRANCH = 3
TOP_K = 2
ROPE_BASE = 10000.0

LANES = 128
SUBLANES = 8
HEAD_PAD = LANES
KV_IN_PAD = 384
ROW_TILE = 256
FFN_CHUNK = 512
FFN_ROWS = 768
ADA_COLS = 1536
MOE_TILE = 896
VMEM_LIMIT = 56 * 1024 * 1024


def _cparams(*sem):
    return pltpu.CompilerParams(dimension_semantics=sem, vmem_limit_bytes=VMEM_LIMIT)


def _const_spec(shape):
    nd = len(shape)
    return pl.BlockSpec(shape, lambda *_: (0,) * nd)


def _sigmoid(x):
    return 0.5 * jnp.tanh(0.5 * x) + 0.5


def _silu(x):
    hx = 0.5 * x
    return hx * jnp.tanh(hx) + hx


def _mod_row(ml_ref, mc_ref, k, is_ctx):
    return jnp.where(is_ctx, mc_ref[0, k:k + 1, :], ml_ref[0, k:k + 1, :])


def _is_ctx(t, tm, ctx_len):
    row = t * tm + lax.broadcasted_iota(jnp.int32, (tm, 1), 0)
    return row < ctx_len


def _rms_mod(x, g, shift, scale):
    y = x * lax.rsqrt(jnp.mean(x * x, axis=-1, keepdims=True) + NORM_EPS) * g
    return y * (1.0 + scale) + shift


def _ada_body(c_ref, w_ref, b_ref, o_ref):
    c = c_ref[...]
    s = _silu(c).astype(BF16)
    o_ref[0] = jnp.dot(s, w_ref[0].astype(BF16), preferred_element_type=F32) + b_ref[0]


def _ada_call(cpad, w_ada, b_ada):
    depth, d, n = w_ada.shape
    r = cpad.shape[0]
    tn = ADA_COLS
    return pl.pallas_call(
        _ada_body,
        grid=(depth, n // tn),
        in_specs=[
            pl.BlockSpec((r, d), lambda l, j: (0, 0)),
            pl.BlockSpec((1, d, tn), lambda l, j: (l, 0, j)),
            pl.BlockSpec((1, 1, tn), lambda l, j: (l, 0, j)),
        ],
        out_specs=pl.BlockSpec((1, r, tn), lambda l, j: (l, 0, j)),
        out_shape=jax.ShapeDtypeStruct((depth, r, n), F32),
        compiler_params=_cparams("arbitrary", "arbitrary"),
        name="ada",
    )(cpad, w_ada, b_ada.reshape(depth, 1, n))


_SEG_WIDTHS = (2 * CONV_CH, Q_LORA, KV_IN_PAD, 2 * REC_WIDTH, N_BRANCH * 1024)
_DOT_COLS = 512


def _inproj_body(x_ref, ml_ref, mc_ref, g_ref, w_ref, *out_refs, tm, ctx_len):
    is_ctx = _is_ctx(pl.program_id(1), tm, ctx_len)
    h = _rms_mod(x_ref[...], g_ref[...], _mod_row(ml_ref, mc_ref, 0, is_ctx), _mod_row(ml_ref, mc_ref, 1, is_ctx))
    hb = h.astype(BF16)
    c0 = 0
    for ref, width in zip(out_refs, _SEG_WIDTHS):
        for j in range(0, width, _DOT_COLS):
            cw = min(_DOT_COLS, width - j)
            ref[:, j:j + cw] = jnp.dot(hb, w_ref[:, c0 + j:c0 + j + cw], preferred_element_type=F32).astype(BF16)
        c0 += width


def _inproj_call(x, ml, mc, g, w, *, nb, t_len, ctx_len):
    m, d = x.shape
    tm = ROW_TILE
    nt = t_len // tm
    row = lambda b, t: (b * nt + t, 0)
    return pl.pallas_call(
        functools.partial(_inproj_body, tm=tm, ctx_len=ctx_len),
        grid=(nb, nt),
        in_specs=[
            pl.BlockSpec((tm, d), row),
            pl.BlockSpec((1, 6, d), lambda b, t: (b, 0, 0)),
            _const_spec((1, 6, d)),
            _const_spec((1, d)),
            pl.BlockSpec(w.shape, lambda b, t: (0, 0), pipeline_mode=pl.Buffered(1)),
        ],
        out_specs=[pl.BlockSpec((tm, wd), row) for wd in _SEG_WIDTHS],
        out_shape=[jax.ShapeDtypeStruct((m, wd), BF16) for wd in _SEG_WIDTHS],
        compiler_params=_cparams("parallel", "parallel"),
        name="inproj",
    )(x, ml, mc, g, w)


_CONV_PAD = 16
_CONV_ROWS = 64


def _conv_body(z_ref, w_ref, b_ref, g_ref, bb_ref, o_ref, u_ref, c_ref, *, t_len, ctx_len):
    ch, pad, rc = CONV_CH, _CONV_PAD, _CONV_ROWS
    zeros = jnp.zeros((pad, ch), F32)
    u_ref[0:pad] = zeros
    u_ref[pad + ctx_len:2 * pad + ctx_len] = zeros
    u_ref[2 * pad + t_len:3 * pad + t_len] = zeros

    def u_row(r0):
        return pl.multiple_of(r0 + pad + jnp.where(r0 >= ctx_len, pad, 0), SUBLANES)

    def glu(i, carry):
        r0 = pl.multiple_of(i * rc, rc)
        z = z_ref[pl.ds(r0, rc), :].astype(F32)
        u_ref[pl.ds(u_row(r0), rc), :] = z[:, :ch] * _sigmoid(z[:, ch:])
        return carry

    lax.fori_loop(0, t_len // rc, glu, 0)

    win = rc + 2 * pad

    def chunk(i, carry):
        r0 = pl.multiple_of(i * rc, rc)
        base = pl.multiple_of(u_row(r0) - pad, SUBLANES)
        for cb in range(ch // LANES):
            ls = slice(cb * LANES, (cb + 1) * LANES)
            w = u_ref[pl.ds(base, win), ls]
            acc = jnp.zeros((rc, LANES), F32)
            for b in range(SUBLANES):
                wb = w if b == 0 else pltpu.roll(w, win - b, axis=0)
                for a in range(win // SUBLANES):
                    k = SUBLANES * a + b - (pad - CONV_WIDTH // 2)
                    if 0 <= k < CONV_WIDTH:
                        acc = acc + w_ref[k:k + 1, ls] * wb[SUBLANES * a:SUBLANES * a + rc]
            c_ref[:, ls] = acc + b_ref[:, ls]
        v = c_ref[...]
        mu = jnp.mean(v, axis=-1, keepdims=True)
        vc = v - mu
        var = jnp.mean(vc * vc, axis=-1, keepdims=True)
        y = vc * lax.rsqrt(var + NORM_EPS) * g_ref[...] + bb_ref[...]
        o_ref[pl.ds(r0, rc), :] = _silu(y).astype(BF16)
        return carry

    lax.fori_loop(0, t_len // rc, chunk, 0)


def _conv_call(zc, w, b, g, bb, *, nb, t_len, ctx_len):
    m = zc.shape[0]
    ch = CONV_CH
    return pl.pallas_call(
        functools.partial(_conv_body, t_len=t_len, ctx_len=ctx_len),
        grid=(nb,),
        in_specs=[
            pl.BlockSpec((t_len, 2 * ch), lambda i: (i, 0)),
            _const_spec((CONV_WIDTH, ch)),
            _const_spec((1, ch)),
            _const_spec((1, ch)),
            _const_spec((1, ch)),
        ],
        out_specs=pl.BlockSpec((t_len, ch), lambda i: (i, 0)),
        out_shape=jax.ShapeDtypeStruct((m, ch), BF16),
        scratch_shapes=[
            pltpu.VMEM((t_len + 3 * _CONV_PAD, ch), F32),
            pltpu.VMEM((_CONV_ROWS, ch), F32),
        ],
        compiler_params=_cparams("parallel"),
        name="conv",
    )(zc, w, b, g, bb)


_REC_ROWS = 128
_REC_PAD = SUBLANES


def _rec_body(z_ref, cw_ref, cb_ref, wa_ref, ba_ref, wi_ref, bi_ref, lam_ref, o_ref, xp_ref, hf_ref, win_ref, *, t_len,
              ctx_len):
    wd, ch, pad = REC_WIDTH, _REC_ROWS, _REC_PAD
    nch = t_len // ch
    nc_ctx = ctx_len // ch
    nblk = ch // SUBLANES
    zeros = jnp.zeros((pad, wd), F32)
    xp_ref[0:pad] = zeros
    xp_ref[pad + ctx_len:2 * pad + ctx_len] = zeros
    xp_ref[2 * pad + t_len:3 * pad + t_len] = zeros

    def xp_row(i):
        return pl.multiple_of(i * ch + pad + jnp.where(i >= nc_ctx, pad, 0), SUBLANES)

    def fill(i, carry):
        r0 = pl.multiple_of(i * ch, ch)
        xp_ref[pl.ds(xp_row(i), ch), :] = z_ref[pl.ds(r0, ch), 0:wd].astype(F32)
        return carry

    lax.fori_loop(0, nch, fill, 0)

    row8 = lax.broadcasted_iota(jnp.int32, (1, SUBLANES, 1), 1)

    def conv4(i, d):
        start = xp_row(i) - (pad if d == 0 else 0)
        win_ref[...] = xp_ref[pl.ds(pl.multiple_of(start, SUBLANES), ch + pad), :]
        lead = pad - (REC_CONV - 1) if d == 0 else 0
        acc = jnp.zeros((ch, wd), F32) + cb_ref[d]
        taps = cw_ref[d]
        for k in range(REC_CONV):
            acc = acc + taps[k:k + 1, :] * win_ref[lead + k:lead + k + ch, :]
        return acc

    def gates(xc, d):
        xb = xc.astype(BF16)
        r = _sigmoid(jnp.dot(xb, wa_ref[d], preferred_element_type=F32) + ba_ref[d])
        ig = _sigmoid(jnp.dot(xb, wi_ref[d], preferred_element_type=F32) + bi_ref[d])
        lam = lam_ref[d]
        softplus_neg = jnp.maximum(-lam, 0.0) + jnp.log1p(jnp.exp(-jnp.abs(lam)))
        log_a = -LRU_C * r * softplus_neg
        a = jnp.exp(log_a)
        bx = jnp.sqrt(-jnp.tanh(log_a) * (1.0 + a * a)) * (ig * xc)
        return a, bx

    def scan_chunk(a, b, carry, d):
        a = a.reshape(nblk, SUBLANES, wd)
        b = b.reshape(nblk, SUBLANES, wd)
        for s in (1, 2, 4):
            shift, keep = (s, row8 >= s) if d == 0 else (SUBLANES - s, row8 < SUBLANES - s)
            a_s, b_s = pltpu.roll(a, shift, axis=1), pltpu.roll(b, shift, axis=1)
            b = jnp.where(keep, a * b_s + b, b)
            a = jnp.where(keep, a * a_s, a)
        outs = [None] * nblk
        for j in (range(nblk) if d == 0 else reversed(range(nblk))):
            hj = b[j] + a[j] * carry
            outs[j] = hj
            carry = hj[SUBLANES - 1:SUBLANES] if d == 0 else hj[0:1]
        return jnp.concatenate(outs, axis=0), carry

    def fwd(i, carry):
        a, bx = gates(conv4(i, 0), 0)
        h, carry = scan_chunk(a, bx, carry, 0)
        hf_ref[pl.ds(pl.multiple_of(i * ch, ch), ch), :] = h
        return carry

    lax.fori_loop(0, nch, fwd, jnp.zeros((1, wd), F32))

    def bwd(j, carry):
        i = jnp.where(j < nc_ctx, nc_ctx - 1 - j, nch - 1 - (j - nc_ctx))
        a, bx = gates(conv4(i, 1), 1)
        h, carry = scan_chunk(a, bx, carry, 1)
        r0 = pl.multiple_of(i * ch, ch)
        gate = z_ref[pl.ds(r0, ch), wd:2 * wd].astype(F32)
        hsum = hf_ref[pl.ds(r0, ch), :] + h
        o_ref[pl.ds(r0, ch), :] = (hsum * jax.nn.gelu(gate)).astype(BF16)
        return carry

    lax.fori_loop(0, nch, bwd, jnp.zeros((1, wd), F32))


def _rec_call(zr, cw, cb, wa, ba, wi, bi, lam, *, nb, t_len, ctx_len):
    m = zr.shape[0]
    wd = REC_WIDTH
    return pl.pallas_call(
        functools.partial(_rec_body, t_len=t_len, ctx_len=ctx_len),
        grid=(nb,),
        in_specs=[
            pl.BlockSpec((t_len, 2 * wd), lambda i: (i, 0)),
            _const_spec((2, REC_CONV, wd)),
            _const_spec((2, 1, wd)),
            _const_spec((2, wd, wd)),
            _const_spec((2, 1, wd)),
            _const_spec((2, wd, wd)),
            _const_spec((2, 1, wd)),
            _const_spec((2, 1, wd)),
        ],
        out_specs=pl.BlockSpec((t_len, wd), lambda i: (i, 0)),
        out_shape=jax.ShapeDtypeStruct((m, wd), BF16),
        scratch_shapes=[
            pltpu.VMEM((t_len + 3 * _REC_PAD, wd), F32),
            pltpu.VMEM((t_len, wd), F32),
            pltpu.VMEM((_REC_ROWS + _REC_PAD, wd), F32),
        ],
        compiler_params=_cparams("parallel"),
        name="rglru",
    )(zr, cw, cb, wa, ba, wi, bi, lam)


def _qkv_body(zq_ref, zkv_ref, ta_ref, tb_ref, gqa_ref, gkva_ref, wq_ref, wkv_ref, gq_ref, gk_ref, q_ref, k_ref, v_ref):
    hp = HEAD_PAD
    zq = zq_ref[...].astype(F32)
    qa = zq * lax.rsqrt(jnp.mean(zq * zq, axis=-1, keepdims=True) + NORM_EPS) * gqa_ref[...]
    q = jnp.dot(qa.astype(BF16), wq_ref[...], preferred_element_type=F32)

    zkv = zkv_ref[...].astype(F32)
    lora = lax.broadcasted_iota(jnp.int32, (1, KV_IN_PAD), 1) < KV_LORA
    ms = jnp.sum(jnp.where(lora, zkv * zkv, 0.0), axis=-1, keepdims=True) * (1.0 / KV_LORA)
    lhs = jnp.where(lora, zkv * lax.rsqrt(ms + NORM_EPS) * gkva_ref[...], zkv)
    kv = jnp.dot(lhs.astype(BF16), wkv_ref[...], preferred_element_type=F32)
    nh = MLA_HEADS * hp
    lane_v = lax.broadcasted_iota(jnp.int32, (1, nh), 1) & (hp - 1)
    v_ref[...] = (kv[:, 2 * nh:] + jnp.where(lane_v == V_HEAD, 1.0, 0.0)).astype(BF16)

    ta, tb = ta_ref[...], tb_ref[...]
    scale = math.log2(math.e) / math.sqrt(QK_HEAD)
    q_a, q_b = ta * gq_ref[0:1, :] * scale, tb * gq_ref[1:2, :] * scale
    k_a, k_b = ta * gk_ref[0:1, :], tb * gk_ref[1:2, :]
    for h in range(MLA_HEADS):
        sl = slice(h * hp, (h + 1) * hp)
        sp = slice(nh + h * hp, nh + (h + 1) * hp)
        for src, fa, fb, ref in ((q, q_a, q_b, q_ref), (kv, k_a, k_b, k_ref)):
            t, tp = src[:, sl], src[:, sp]
            rs = lax.rsqrt(jnp.sum(t * t, axis=-1, keepdims=True) * (1.0 / QK_HEAD) + NORM_EPS)
            ref[:, sl] = (rs * (t * fa + tp * fb)).astype(BF16)


def _qkv_call(zq, zkv, ta, tb, gqa, gkva, wq, wkv, gq, gk, *, nb, t_len):
    m = zq.shape[0]
    tm = ROW_TILE
    nt = t_len // tm
    row = lambda b, t: (b * nt + t, 0)
    pos = lambda b, t: (t, 0)
    nq = MLA_HEADS * HEAD_PAD
    nv = nq
    return pl.pallas_call(
        _qkv_body,
        grid=(nb, nt),
        in_specs=[
            pl.BlockSpec((tm, Q_LORA), row),
            pl.BlockSpec((tm, KV_IN_PAD), row),
            pl.BlockSpec((tm, HEAD_PAD), pos),
            pl.BlockSpec((tm, HEAD_PAD), pos),
            _const_spec((1, Q_LORA)),
            _const_spec((1, KV_IN_PAD)),
            _const_spec(wq.shape),
            _const_spec(wkv.shape),
            _const_spec((2, HEAD_PAD)),
            _const_spec((2, HEAD_PAD)),
        ],
        out_specs=[pl.BlockSpec((tm, nq), row), pl.BlockSpec((tm, nq), row), pl.BlockSpec((tm, nv), row)],
        out_shape=[jax.ShapeDtypeStruct((m, nq), BF16), jax.ShapeDtypeStruct((m, nq), BF16),
                   jax.ShapeDtypeStruct((m, nv), BF16)],
        compiler_params=_cparams("parallel", "parallel"),
        name="qkv",
    )(zq, zkv, ta, tb, gqa, gkva, wq, wkv, gq, gk)


def _attn_body(q_ref, k_ref, v_ref, o_ref, *, t_len, ctx_len):
    hp = HEAD_PAD
    low = lax.broadcasted_iota(jnp.int32, (1, hp), 1) < V_HEAD

    def attend(nk, ahead):
        def scores(h):
            sl = slice(h * hp, (h + 1) * hp)
            return lax.dot_general(q_ref[:, sl], k_ref[0:nk, sl], (((1,), (1,)), ((), ())),
                                   preferred_element_type=F32)

        queue = [scores(h) for h in range(min(ahead, MLA_HEADS))]
        outs = []
        for h in range(MLA_HEADS):
            s = queue.pop(0)
            if h + ahead < MLA_HEADS:
                queue.append(scores(h + ahead))
            p = jnp.exp2(s - jnp.max(s, axis=-1, keepdims=True)).astype(BF16)
            pv = jnp.dot(p, v_ref[0:nk, h * hp:(h + 1) * hp], preferred_element_type=F32)
            outs.append(pv / pv[:, V_HEAD:V_HEAD + 1])
            if h % 2 == 1:
                both = jnp.where(low, outs[h - 1], pltpu.roll(outs[h], V_HEAD, axis=1))
                o_ref[:, (h // 2) * hp:(h // 2 + 1) * hp] = both.astype(BF16)

    is_ctx_tile = pl.program_id(1) * ROW_TILE < ctx_len

    @pl.when(is_ctx_tile)
    def _():
        attend(ctx_len, ahead=MLA_HEADS)

    @pl.when(jnp.logical_not(is_ctx_tile))
    def _():
        attend(t_len, ahead=2)


def _attn_call(q, k, v, *, nb, t_len, ctx_len):
    m = q.shape[0]
    tq = ROW_TILE
    nt = t_len // tq
    nq = MLA_HEADS * HEAD_PAD
    nv = MLA_HEADS * V_HEAD
    return pl.pallas_call(
        functools.partial(_attn_body, t_len=t_len, ctx_len=ctx_len),
        grid=(nb, nt),
        in_specs=[
            pl.BlockSpec((tq, nq), lambda b, t: (b * nt + t, 0)),
            pl.BlockSpec((t_len, nq), lambda b, t: (b, 0)),
            pl.BlockSpec((t_len, nq), lambda b, t: (b, 0)),
        ],
        out_specs=pl.BlockSpec((tq, nv), lambda b, t: (b * nt + t, 0)),
        out_shape=jax.ShapeDtypeStruct((m, nv), BF16),
        compiler_params=_cparams("parallel", "arbitrary"),
        name="attn",
    )(q, k, v)


_ROUTER_LOW_LANE = 64


def _merge_body(ca_ref, ao_ref, rr_ref, zg_ref, x_ref, ml_ref, mc_ref, bg_ref, g2_ref, wc_ref, wm_ref, wr_ref,
                wo_ref, *rest, tm, ctx_len, n_exp):
    moe = n_exp > 0
    d = x_ref.shape[-1]
    is_ctx = _is_ctx(pl.program_id(1), tm, ctx_len)
    merged = jnp.zeros((tm, d), F32)
    for j, (src, w) in enumerate(((ca_ref, wc_ref), (ao_ref, wm_ref), (rr_ref, wr_ref))):
        gate = _sigmoid(zg_ref[:, j * d:(j + 1) * d].astype(F32) + bg_ref[:, j * d:(j + 1) * d])
        merged = merged + gate * jnp.dot(src[...], w[...], preferred_element_type=F32)
    y = jnp.dot(merged.astype(BF16), wo_ref[...], preferred_element_type=F32)
    x = x_ref[...] + _mod_row(ml_ref, mc_ref, 2, is_ctx) * y
    h2 = _rms_mod(x, g2_ref[...], _mod_row(ml_ref, mc_ref, 3, is_ctx), _mod_row(ml_ref, mc_ref, 4, is_ctx))
    if not moe:
        xo_ref, h_ref = rest
        xo_ref[...] = x
        h_ref[...] = h2.astype(BF16)
        return
    wrt_ref, xo_ref, h_ref, ei_ref, ew_ref = rest
    xo_ref[...] = x
    for s in range(d // LANES):
        h_ref[pl.ds(s, tm, stride=SUBLANES), :] = h2[:, s * LANES:(s + 1) * LANES]
    h_hi = h2.astype(BF16)
    h_lo = (h2 - h_hi.astype(F32)).astype(BF16)
    part = (jnp.dot(h_hi, wrt_ref[...], preferred_element_type=F32)
            + jnp.dot(h_lo, wrt_ref[...], preferred_element_type=F32))
    logits = part + pltpu.roll(part, LANES - _ROUTER_LOW_LANE, axis=1)
    lane = lax.broadcasted_iota(jnp.int32, (tm, LANES), 1).astype(F32)
    logits = jnp.where(lane < n_exp, logits, -jnp.inf)
    m1 = jnp.max(logits, axis=-1, keepdims=True)
    i1 = jnp.min(jnp.where(logits == m1, lane, float(LANES)), axis=-1, keepdims=True)
    rest_l = jnp.where(lane == i1, -jnp.inf, logits)
    m2 = jnp.max(rest_l, axis=-1, keepdims=True)
    i2 = jnp.min(jnp.where(rest_l == m2, lane, float(LANES)), axis=-1, keepdims=True)
    e2 = jnp.exp(m2 - m1)
    w1 = 1.0 / (1.0 + e2)
    w2 = e2 / (1.0 + e2)
    ei_ref[...] = jnp.where(lane == 0.0, i1, jnp.where(lane == 1.0, i2, 0.0)).astype(jnp.int32)
    ew_ref[...] = jnp.where(lane == 0.0, w1, jnp.where(lane == 1.0, w2, 0.0))


def _merge_call(ca, ao, rr, zg, x, ml, mc, bg, g2, wc, wm, wr, wo, w_router, *, nb, t_len, ctx_len):
    m, d = x.shape
    tm = ROW_TILE
    nt = t_len // tm
    row = lambda b, t: (b * nt + t, 0)
    moe = w_router is not None
    n_exp = w_router.shape[1] if moe else 0
    wrt = None
    if moe:
        assert n_exp <= _ROUTER_LOW_LANE
        w_hi = w_router.astype(BF16)
        w_lo = (w_router - w_hi.astype(F32)).astype(BF16)
        wrt = jnp.zeros((d, LANES), BF16).at[:, :n_exp].set(w_hi).at[:, _ROUTER_LOW_LANE:_ROUTER_LOW_LANE + n_exp].set(w_lo)
    in_specs = [
        pl.BlockSpec((tm, ca.shape[1]), row),
        pl.BlockSpec((tm, ao.shape[1]), row),
        pl.BlockSpec((tm, rr.shape[1]), row),
        pl.BlockSpec((tm, zg.shape[1]), row),
        pl.BlockSpec((tm, d), row),
        pl.BlockSpec((1, 6, d), lambda b, t: (b, 0, 0)),
        _const_spec((1, 6, d)),
        _const_spec(bg.shape),
        _const_spec(g2.shape),
        _const_spec(wc.shape),
        _const_spec(wm.shape),
        _const_spec(wr.shape),
        _const_spec(wo.shape),
    ]
    args = [ca, ao, rr, zg, x, ml, mc, bg, g2, wc, wm, wr, wo]
    out_specs = [pl.BlockSpec((tm, d), row)]
    out_shape = [jax.ShapeDtypeStruct((m, d), F32)]
    if moe:
        in_specs.append(_const_spec(wrt.shape))
        args.append(wrt)
        out_specs += [pl.BlockSpec((tm * SUBLANES, LANES), row), pl.BlockSpec((tm, LANES), row),
                      pl.BlockSpec((tm, LANES), row)]
        out_shape += [jax.ShapeDtypeStruct((m * SUBLANES, LANES), F32), jax.ShapeDtypeStruct((m, LANES), jnp.int32),
                      jax.ShapeDtypeStruct((m, LANES), F32)]
    else:
        out_specs.append(pl.BlockSpec((tm, d), row))
        out_shape.append(jax.ShapeDtypeStruct((m, d), BF16))
    return pl.pallas_call(
        functools.partial(_merge_body, tm=tm, ctx_len=ctx_len, n_exp=n_exp),
        grid=(nb, nt),
        in_specs=in_specs,
        out_specs=out_specs,
        out_shape=out_shape,
        input_output_aliases={4: 0},
        compiler_params=_cparams("parallel", "parallel"),
        name="merge_moe" if moe else "merge",
    )(*args)


def _ffn_body(h_ref, x_ref, ml_ref, mc_ref, wg_ref, wu_ref, wd_ref, o_ref, acc_ref, *, tm, ctx_len):
    f = pl.program_id(2)

    @pl.when(f == 0)
    def _():
        acc_ref[...] = jnp.zeros_like(acc_ref)

    h = h_ref[...]
    g = jnp.dot(h, wg_ref[0], preferred_element_type=F32)
    u = jnp.dot(h, wu_ref[0], preferred_element_type=F32)
    acc_ref[...] += jnp.dot((_silu(g) * u).astype(BF16), wd_ref[...], preferred_element_type=F32)

    @pl.when(f == pl.num_programs(2) - 1)
    def _():
        is_ctx = _is_ctx(pl.program_id(1), tm, ctx_len)
        o_ref[...] = x_ref[...] + _mod_row(ml_ref, mc_ref, 5, is_ctx) * acc_ref[...]


def _ffn_call(h, x, ml, mc, wg, wu, wd, *, nb, t_len, ctx_len):
    m, d = x.shape
    nf, _, tf = wg.shape
    tm = FFN_ROWS if t_len % FFN_ROWS == 0 else ROW_TILE
    nt = t_len // tm
    row = lambda b, t, f: (b * nt + t, 0)
    return pl.pallas_call(
        functools.partial(_ffn_body, tm=tm, ctx_len=ctx_len),
        grid=(nb, nt, nf),
        in_specs=[
            pl.BlockSpec((tm, d), row),
            pl.BlockSpec((tm, d), row),
            pl.BlockSpec((1, 6, d), lambda b, t, f: (b, 0, 0)),
            pl.BlockSpec((1, 6, d), lambda b, t, f: (0, 0, 0)),
            pl.BlockSpec((1, d, tf), lambda b, t, f: (f, 0, 0)),
            pl.BlockSpec((1, d, tf), lambda b, t, f: (f, 0, 0)),
            pl.BlockSpec((tf, d), lambda b, t, f: (f, 0)),
        ],
        out_specs=pl.BlockSpec((tm, d), row),
        out_shape=jax.ShapeDtypeStruct((m, d), F32),
        scratch_shapes=[pltpu.VMEM((tm, d), F32)],
        input_output_aliases={1: 0},
        compiler_params=_cparams("parallel", "parallel", "arbitrary"),
        name="ffn",
    )(h, x, ml, mc, wg, wu, wd)


def _gather_tiles(idx_ref, src_hbm, dst_ref, slot, sem, n):
    def issue(r, carry):
        src = pl.multiple_of(idx_ref[0, 0, r] * SUBLANES, SUBLANES)
        pltpu.make_async_copy(src_hbm.at[pl.ds(src, SUBLANES), :],
                              dst_ref.at[slot, pl.ds(pl.multiple_of(r * SUBLANES, SUBLANES), SUBLANES), :],
                              sem.at[slot]).start()
        return carry

    lax.fori_loop(0, n, issue, 0, unroll=8)


def _wait_tiles(src_hbm, dst_ref, slot, sem, n):
    pltpu.make_async_copy(src_hbm.at[pl.ds(0, n * SUBLANES), :], dst_ref.at[slot], sem.at[slot]).wait()


def _moe_body(te_ref, nu_ref, st_ref, stn_ref, h_hbm, wg_ref, wu_ref, wd_ref, y_ref, xt_ref, xb_ref, acc_ref, sem, *,
              tm, nf_static):
    i, f = pl.program_id(0), pl.program_id(1)
    nf = pl.num_programs(1)
    d = xb_ref.shape[1]
    n_used = nu_ref[0]
    valid = i < n_used
    slot = i & 1

    @pl.when(jnp.logical_and(i == 0, f == 0))
    def _():
        _gather_tiles(st_ref, h_hbm, xt_ref, 0, sem, tm)

    @pl.when(jnp.logical_and(i <= n_used, f == 0))
    def _():
        _wait_tiles(h_hbm, xt_ref, slot, sem, tm)

    @pl.when(jnp.logical_and(valid, f == 0))
    def _():
        for s in range(d // LANES):
            xb_ref[:, s * LANES:(s + 1) * LANES] = xt_ref[slot, pl.ds(s, tm, stride=SUBLANES), :].astype(BF16)
        acc_ref[...] = jnp.zeros_like(acc_ref)

    @pl.when(valid)
    def _():
        per_step = tm // nf_static
        for r in range(per_step):
            rr = f * per_step + r
            src = pl.multiple_of(stn_ref[0, 0, rr] * SUBLANES, SUBLANES)
            pltpu.make_async_copy(h_hbm.at[pl.ds(src, SUBLANES), :],
                                  xt_ref.at[1 - slot, pl.ds(pl.multiple_of(rr * SUBLANES, SUBLANES), SUBLANES), :],
                                  sem.at[1 - slot]).start()
        x = xb_ref[...]
        g = jnp.dot(x, wg_ref[0, 0], preferred_element_type=F32)
        u = jnp.dot(x, wu_ref[0, 0], preferred_element_type=F32)
        acc_ref[...] += jnp.dot((_silu(g) * u).astype(BF16), wd_ref[0], preferred_element_type=F32)

    @pl.when(jnp.logical_and(valid, f == nf - 1))
    def _():
        for s in range(d // LANES):
            y_ref[pl.ds(s, tm, stride=SUBLANES), :] = acc_ref[:, s * LANES:(s + 1) * LANES]

    @pl.when(jnp.logical_and(jnp.logical_not(valid), f == nf - 1))
    def _():
        y_ref[...] = jnp.zeros_like(y_ref)


def _moe_call(tile_expert, n_used, slot_token, h_tiles, wg, wu, wd):
    n_exp, nf, d, tf = wg.shape
    tm = MOE_TILE
    assert tm % nf == 0
    n_tiles = slot_token.shape[0]

    def w_in_map(i, f, te, nu):
        return (te[i], jnp.where(i < nu[0], f, nf - 1), 0, 0)

    def w_out_map(i, f, te, nu):
        return (te[i], jnp.where(i < nu[0], f, nf - 1), 0)

    grid_spec = pltpu.PrefetchScalarGridSpec(
        num_scalar_prefetch=2,
        grid=(n_tiles, nf),
        in_specs=[
            pl.BlockSpec((1, 1, tm), lambda i, f, te, nu: (i, 0, 0), memory_space=pltpu.SMEM),
            pl.BlockSpec((1, 1, tm), lambda i, f, te, nu: (jnp.minimum(i + 1, n_tiles - 1), 0, 0),
                         memory_space=pltpu.SMEM),
            pl.BlockSpec(memory_space=pl.ANY),
            pl.BlockSpec((1, 1, d, tf), w_in_map),
            pl.BlockSpec((1, 1, d, tf), w_in_map),
            pl.BlockSpec((1, tf, d), w_out_map),
        ],
        out_specs=pl.BlockSpec((tm * SUBLANES, LANES), lambda i, f, te, nu: (i, 0)),
        scratch_shapes=[
            pltpu.VMEM((2, tm * SUBLANES, LANES), F32),
            pltpu.VMEM((tm, d), BF16),
            pltpu.VMEM((tm, d), F32),
            pltpu.SemaphoreType.DMA((2,)),
        ],
    )
    return pl.pallas_call(
        functools.partial(_moe_body, tm=tm, nf_static=nf),
        grid_spec=grid_spec,
        out_shape=jax.ShapeDtypeStruct((n_tiles * tm * SUBLANES, LANES), F32),
        compiler_params=_cparams("arbitrary", "arbitrary"),
        name="moe",
    )(tile_expert, n_used, slot_token, slot_token, h_tiles, wg, wu, wd)


def _combine_body(p1_ref, p2_ref, p1n_ref, p2n_ref, y_hbm, ew_ref, x_ref, ml_ref, mc_ref, o_ref, y1_ref, y2_ref, sem,
                  *, tm, ctx_len, tile_of, nt):
    n = pl.program_id(0)
    d = x_ref.shape[1]
    slot = n & 1

    @pl.when(n == 0)
    def _():
        _gather_tiles(p1_ref, y_hbm, y1_ref, 0, sem.at[0], tm)
        _gather_tiles(p2_ref, y_hbm, y2_ref, 0, sem.at[1], tm)

    @pl.when(n + 1 < pl.num_programs(0))
    def _():
        _gather_tiles(p1n_ref, y_hbm, y1_ref, 1 - slot, sem.at[0], tm)
        _gather_tiles(p2n_ref, y_hbm, y2_ref, 1 - slot, sem.at[1], tm)

    _wait_tiles(y_hbm, y1_ref, slot, sem.at[0], tm)
    _wait_tiles(y_hbm, y2_ref, slot, sem.at[1], tm)
    is_ctx = _is_ctx(tile_of(n) % nt, tm, ctx_len)
    w1, w2 = ew_ref[:, 0:1], ew_ref[:, 1:2]
    ga = _mod_row(ml_ref, mc_ref, 5, is_ctx)
    for s in range(d // LANES):
        ls = slice(s * LANES, (s + 1) * LANES)
        f = (w1 * y1_ref[slot, pl.ds(s, tm, stride=SUBLANES), :]
             + w2 * y2_ref[slot, pl.ds(s, tm, stride=SUBLANES), :])
        o_ref[:, ls] = x_ref[:, ls] + ga[:, ls] * f


def _combine_call(pos1, pos2, y_tiles, ew, x, ml, mc, *, nb, t_len, ctx_len, latent_only):
    m, d = x.shape
    tm = ROW_TILE
    nt = t_len // tm
    nc = ctx_len // tm if latent_only else 0
    per = nt - nc
    n_steps = nb * per

    def tile_of(n):
        return (n // per) * nt + nc + n % per

    cur = lambda n: (n, 0, 0)
    nxt = lambda n: (jnp.minimum(n + 1, n_steps - 1), 0, 0)
    row = lambda n: (tile_of(n), 0)
    return pl.pallas_call(
        functools.partial(_combine_body, tm=tm, ctx_len=ctx_len, tile_of=tile_of, nt=nt),
        grid=(n_steps,),
        in_specs=[
            pl.BlockSpec((1, 1, tm), cur, memory_space=pltpu.SMEM),
            pl.BlockSpec((1, 1, tm), cur, memory_space=pltpu.SMEM),
            pl.BlockSpec((1, 1, tm), nxt, memory_space=pltpu.SMEM),
            pl.BlockSpec((1, 1, tm), nxt, memory_space=pltpu.SMEM),
            pl.BlockSpec(memory_space=pl.ANY),
            pl.BlockSpec((tm, LANES), row),
            pl.BlockSpec((tm, d), row),
            pl.BlockSpec((1, 6, d), lambda n: (n // per, 0, 0)),
            _const_spec((1, 6, d)),
        ],
        out_specs=pl.BlockSpec((tm, d), lambda n: (n, 0)),
        out_shape=jax.ShapeDtypeStruct((n_steps * tm, d), F32),
        scratch_shapes=[
            pltpu.VMEM((2, tm * SUBLANES, LANES), F32),
            pltpu.VMEM((2, tm * SUBLANES, LANES), F32),
            pltpu.SemaphoreType.DMA((2, 2)),
        ],
        input_output_aliases={} if latent_only else {6: 0},
        compiler_params=_cparams("arbitrary"),
        name="moe_combine",
    )(pos1, pos2, pos1, pos2, y_tiles, ew, x, ml, mc)


def _route_tables(eidx, token_rows, n_exp, tile):
    m = eidx.shape[0]
    e = eidx.reshape(-1)
    onehot = (e[:, None] == jnp.arange(n_exp, dtype=jnp.int32)[None, :]).astype(jnp.int32)
    csum = jnp.cumsum(onehot, axis=0)
    rank = jnp.sum(csum * onehot, axis=1) - 1
    tiles_per = (csum[-1] + tile - 1) // tile
    tile_end = jnp.cumsum(tiles_per)
    tile_start = tile_end - tiles_per
    pos = jnp.sum(onehot * tile_start[None, :], axis=1) * tile + rank
    n_tiles = -(-(TOP_K * m) // tile) + n_exp
    tidx = jnp.arange(n_tiles, dtype=jnp.int32)
    n_used = tile_end[-1].astype(jnp.int32)
    texp = jnp.sum((tidx[:, None] >= tile_end[None, :]).astype(jnp.int32), axis=1)
    last = jnp.sum((n_used - 1 >= tile_end).astype(jnp.int32))
    texp = jnp.where(tidx < n_used, texp, last).astype(jnp.int32)
    slot_token = jnp.zeros((n_tiles * tile,), jnp.int32).at[pos].set(jnp.repeat(token_rows, TOP_K),
                                                                   unique_indices=True)
    pos = pos.reshape(m, TOP_K).astype(jnp.int32)
    return texp, n_used.reshape(1), slot_token.reshape(n_tiles, 1, tile), pos[:, 0], pos[:, 1]


def _rope_tables(seq, ctx_len):
    rows = seq // GRID_W
    row = jnp.repeat(jnp.arange(rows, dtype=jnp.int32), GRID_W).astype(F32)
    col = jnp.tile(jnp.arange(GRID_W, dtype=jnp.int32), rows).astype(F32)
    half = QK_ROPE // 2
    freqs = ROPE_BASE ** (-jnp.arange(0, half, 2, dtype=F32) / half)
    ar, ac = row[:, None] * freqs, col[:, None] * freqs
    cos = jnp.concatenate([jnp.cos(ar), jnp.cos(ar), jnp.cos(ac), jnp.cos(ac)], axis=1)
    sin = jnp.concatenate([-jnp.sin(ar), jnp.sin(ar), -jnp.sin(ac), jnp.sin(ac)], axis=1)
    ones = jnp.ones((seq, QK_NOPE), F32)
    ta = jnp.concatenate([ones, cos, jnp.ones((seq, HEAD_PAD - QK_HEAD), F32)], axis=1)
    tb = jnp.concatenate([0 * ones, sin, jnp.zeros((seq, HEAD_PAD - QK_HEAD), F32)], axis=1)
    ta = jnp.concatenate([jnp.ones((ctx_len, HEAD_PAD), F32), ta], axis=0)
    tb = jnp.concatenate([jnp.zeros((ctx_len, HEAD_PAD), F32), tb], axis=0)
    return ta, tb


def _rope_partner():
    q = QK_ROPE // 4
    return jnp.array(list(range(q, 2 * q)) + list(range(0, q)) + list(range(3 * q, 4 * q)) + list(range(2 * q, 3 * q)),
                     dtype=jnp.int32)


def _head_gains(g):
    perm = _rope_partner()
    zeros = jnp.zeros((HEAD_PAD - QK_HEAD,), F32)
    g_a = jnp.concatenate([g, zeros])
    g_b = jnp.concatenate([jnp.zeros((QK_NOPE,), F32), g[QK_NOPE:][perm], zeros])
    return jnp.stack([g_a, g_b])


def _pack_wq(w_q_b):
    perm = _rope_partner()
    w = w_q_b.reshape(Q_LORA, MLA_HEADS, QK_HEAD)
    tail = jnp.zeros((Q_LORA, MLA_HEADS, HEAD_PAD - QK_HEAD), F32)
    main = jnp.concatenate([w, tail], axis=-1)
    partner = jnp.concatenate([jnp.zeros((Q_LORA, MLA_HEADS, QK_NOPE), F32), w[:, :, QK_NOPE:][:, :, perm], tail], axis=-1)
    nh = MLA_HEADS * HEAD_PAD
    return jnp.concatenate([main.reshape(Q_LORA, nh), partner.reshape(Q_LORA, nh)], axis=1).astype(BF16)


def _pack_wkv(w_kv_b):
    perm = _rope_partner()
    w = w_kv_b.reshape(KV_LORA, MLA_HEADS, QK_NOPE + V_HEAD)
    eye = jnp.eye(QK_ROPE, dtype=F32)
    left = jnp.zeros((QK_ROPE, QK_NOPE), F32)
    right = jnp.zeros((QK_ROPE, HEAD_PAD - QK_HEAD), F32)
    nh = MLA_HEADS * HEAD_PAD

    def per_head(rope_rows):
        return jnp.broadcast_to(rope_rows[:, None, :], (QK_ROPE, MLA_HEADS, HEAD_PAD)).reshape(QK_ROPE, nh)

    k_top = jnp.concatenate([w[:, :, :QK_NOPE], jnp.zeros((KV_LORA, MLA_HEADS, HEAD_PAD - QK_NOPE), F32)], axis=-1)
    main = jnp.concatenate([k_top.reshape(KV_LORA, nh), per_head(jnp.concatenate([left, eye, right], axis=1))], axis=0)
    partner = jnp.concatenate([jnp.zeros((KV_LORA, nh), F32),
                               per_head(jnp.concatenate([left, eye[:, perm], right], axis=1))], axis=0)
    v_top = jnp.concatenate([w[:, :, QK_NOPE:], jnp.zeros((KV_LORA, MLA_HEADS, HEAD_PAD - V_HEAD), F32)], axis=-1)
    v_cols = jnp.concatenate([v_top.reshape(KV_LORA, nh), jnp.zeros((QK_ROPE, nh), F32)], axis=0)
    full = jnp.concatenate([main, partner, v_cols], axis=1)
    return jnp.pad(full, ((0, KV_IN_PAD - KV_LORA - QK_ROPE), (0, 0))).astype(BF16)


def _pack_w_in(w_in):
    o1 = 2 * CONV_CH
    o2 = o1 + Q_LORA
    o3 = o2 + KV_LORA + QK_ROPE
    kv = jnp.pad(w_in[:, o2:o3], ((0, 0), (0, KV_IN_PAD - KV_LORA - QK_ROPE)))
    return jnp.concatenate([w_in[:, :o2], kv, w_in[:, o3:]], axis=1).astype(BF16)


def _chunk_cols(w):
    *lead, d, dff = w.shape
    w = w.reshape(*lead, d, dff // FFN_CHUNK, FFN_CHUNK)
    return jnp.swapaxes(w, -3, -2).astype(BF16)


def _block_diag(w):
    nd, nblk, bw, _ = w.shape
    eye = jnp.eye(nblk, dtype=w.dtype)
    return jnp.einsum("dgij,gh->dgihj", w, eye).reshape(nd, nblk * bw, nblk * bw).astype(BF16)


def kernel(x, c, ctx, c_ctx, w_ada, b_ada, g_norm1, g_norm2, w_in, b_gate, conv_w, conv_b, conv_ln_g, conv_ln_b, w_o_conv, g_q_a, w_q_b, g_kv_a, w_kv_b, g_qn, g_kn, w_o_mla, rec_conv_w, rec_conv_b, w_ra, b_ra, w_ri, b_ri, lru_lambda, w_o_rec, w_out, w_ff_gate, w_ff_up, w_ff_down, w_router, w_e_gate, w_e_up, w_e_down):
    nb, seq, d = x.shape
    ctx_len = ctx.shape[1]
    depth = w_ada.shape[0]
    t_len = ctx_len + seq
    m = nb * t_len
    n_exp = w_router.shape[-1]
    assert d == 1024 and ctx_len % ROW_TILE == 0 and seq % ROW_TILE == 0 and seq % GRID_W == 0
    dims =dict(nb=nb, t_len=t_len, ctx_len=ctx_len)

    r_pad = -(-(nb + 1) // SUBLANES) * SUBLANES
    cpad = jnp.zeros((r_pad, d), F32).at[:nb].set(c).at[nb].set(c_ctx)
    mods = _ada_call(cpad, w_ada, b_ada).reshape(depth, r_pad, 6, d)

    ta, tb = _rope_tables(seq, ctx_len)
    xs = jnp.concatenate([ctx, x], axis=1).reshape(m, d)

    for i in range(depth):
        moe = i % 2 == 1
        j = i // 2
        ml, mc = mods[i, :nb], mods[i, nb:nb + 1]
        zc, zq, zkv, zr, zg = _inproj_call(xs, ml, mc, g_norm1[i][None], _pack_w_in(w_in[i]), **dims)
        ca = _conv_call(zc, conv_w[i], conv_b[i][None], conv_ln_g[i][None], conv_ln_b[i][None], **dims)
        rr = _rec_call(zr, rec_conv_w[i], rec_conv_b[i][:, None], _block_diag(w_ra[i]), b_ra[i][:, None],
                       _block_diag(w_ri[i]), b_ri[i][:, None], lru_lambda[i][:, None], **dims)
        gkva = jnp.concatenate([g_kv_a[i], jnp.ones((KV_IN_PAD - KV_LORA,), F32)])[None]
        q, k, v = _qkv_call(zq, zkv, ta, tb, g_q_a[i][None], gkva, _pack_wq(w_q_b[i]), _pack_wkv(w_kv_b[i]),
                            _head_gains(g_qn[i]), _head_gains(g_kn[i]), nb=nb, t_len=t_len)
        ao = _attn_call(q, k, v, **dims)
        wrt = w_router[j] if moe else None
        outs = _merge_call(ca, ao, rr, zg, xs, ml, mc, b_gate[i][None], g_norm2[i][None], w_o_conv[i].astype(BF16),
                           w_o_mla[i].astype(BF16), w_o_rec[i].astype(BF16), w_out[i].astype(BF16), wrt, **dims)
        if not moe:
            xs, h2 = outs
            xs = _ffn_call(h2, xs, ml, mc, _chunk_cols(w_ff_gate[j]), _chunk_cols(w_ff_up[j]),
                           w_ff_down[j].astype(BF16), **dims)
            continue
        xs, h_tiles, eidx, ew = outs
        latent_only = i == depth - 1
        rows = jnp.arange(m, dtype=jnp.int32).reshape(nb, t_len)
        experts = eidx[:, :TOP_K].reshape(nb, t_len, TOP_K)
        if latent_only:
            rows, experts = rows[:, ctx_len:], experts[:, ctx_len:]
        rows = rows.reshape(-1)
        texp, n_used, slot_token, pos1, pos2 = _route_tables(experts.reshape(-1, TOP_K), rows, n_exp, MOE_TILE)
        y_tiles = _moe_call(texp, n_used, slot_token, h_tiles, _chunk_cols(w_e_gate[j]), _chunk_cols(w_e_up[j]),
                            w_e_down[j].astype(BF16))
        nrt = rows.shape[0] // ROW_TILE
        xs = _combine_call(pos1.reshape(nrt, 1, ROW_TILE), pos2.reshape(nrt, 1, ROW_TILE), y_tiles, ew, xs, ml, mc,
                           latent_only=latent_only, **dims)
        if latent_only:
            return xs.reshape(nb, seq, d)
    return xs.reshape(nb, t_len, d)[:, ctx_len:]
```

```python
import functools
import math

import jax
import jax.numpy as jnp
from jax import lax
from jax.experimental import pallas as pl
from jax.experimental.pallas import tpu as pltpu

F32 = jnp.float32
BF16 = jnp.bfloat16

NORM_EPS = 1e-6
GRID_W = 64
CONV_CH = 512
CONV_WIDTH = 31
MLA_HEADS = 8
QK_NOPE = 64
QK_ROPE = 32
V_HEAD = 64
Q_LORA = 768
KV_LORA = 256
QK_HEAD = QK_NOPE + QK_ROPE
REC_WIDTH = 512
REC_CONV = 4
LRU_C = 8.0
N_BRANCH = 3
TOP_K = 2
ROPE_BASE = 10000.0

LANES = 128
SUBLANES = 8
HEAD_PAD = LANES
KV_IN_PAD = 384
ROW_TILE = 256
FFN_CHUNK = 512
FFN_ROWS = 768
ADA_COLS = 1536
MOE_TILE = 896
VMEM_LIMIT = 56 * 1024 * 1024
MOE_VMEM_LIMIT = 60 * 1024 * 1024


def _cparams(*sem):
    return pltpu.CompilerParams(dimension_semantics=sem, vmem_limit_bytes=VMEM_LIMIT)


def _const_spec(shape):
    nd = len(shape)
    return pl.BlockSpec(shape, lambda *_: (0,) * nd)


def _sigmoid(x):
    return 0.5 * jnp.tanh(0.5 * x) + 0.5


def _silu(x):
    hx = 0.5 * x
    return hx * jnp.tanh(hx) + hx


def _mod_row(ml_ref, mc_ref, k, is_ctx):
    return jnp.where(is_ctx, mc_ref[0, k:k + 1, :], ml_ref[0, k:k + 1, :])


def _is_ctx(t, tm, ctx_len):
    row = t * tm + lax.broadcasted_iota(jnp.int32, (tm, 1), 0)
    return row < ctx_len


def _rms_mod(x, g, shift, scale):
    y = x * lax.rsqrt(jnp.mean(x * x, axis=-1, keepdims=True) + NORM_EPS) * g
    return y * (1.0 + scale) + shift


def _ada_body(c_ref, w_ref, b_ref, o_ref):
    c = c_ref[...]
    s = _silu(c).astype(BF16)
    o_ref[0] = jnp.dot(s, w_ref[0].astype(BF16), preferred_element_type=F32) + b_ref[0]


def _ada_call(cpad, w_ada, b_ada):
    depth, d, n = w_ada.shape
    r = cpad.shape[0]
    tn = ADA_COLS
    return pl.pallas_call(
        _ada_body,
        grid=(depth, n // tn),
        in_specs=[
            pl.BlockSpec((r, d), lambda l, j: (0, 0)),
            pl.BlockSpec((1, d, tn), lambda l, j: (l, 0, j)),
            pl.BlockSpec((1, 1, tn), lambda l, j: (l, 0, j)),
        ],
        out_specs=pl.BlockSpec((1, r, tn), lambda l, j: (l, 0, j)),
        out_shape=jax.ShapeDtypeStruct((depth, r, n), F32),
        compiler_params=_cparams("arbitrary", "arbitrary"),
        name="ada",
    )(cpad, w_ada, b_ada.reshape(depth, 1, n))


_SEG_WIDTHS = (2 * CONV_CH, Q_LORA, KV_IN_PAD, 2 * REC_WIDTH, N_BRANCH * 1024)
_DOT_COLS = 512


def _inproj_body(x_ref, ml_ref, mc_ref, g_ref, w_ref, *out_refs, tm, ctx_len):
    is_ctx = _is_ctx(pl.program_id(1), tm, ctx_len)
    h = _rms_mod(x_ref[...], g_ref[...], _mod_row(ml_ref, mc_ref, 0, is_ctx), _mod_row(ml_ref, mc_ref, 1, is_ctx))
    hb = h.astype(BF16)
    c0 = 0
    for ref, width in zip(out_refs, _SEG_WIDTHS):
        for j in range(0, width, _DOT_COLS):
            cw = min(_DOT_COLS, width - j)
            ref[:, j:j + cw] = jnp.dot(hb, w_ref[:, c0 + j:c0 + j + cw], preferred_element_type=F32).astype(BF16)
        c0 += width


def _inproj_call(x, ml, mc, g, w, *, nb, t_len, ctx_len):
    m, d = x.shape
    tm = ROW_TILE
    nt = t_len // tm
    row = lambda b, t: (b * nt + t, 0)
    return pl.pallas_call(
        functools.partial(_inproj_body, tm=tm, ctx_len=ctx_len),
        grid=(nb, nt),
        in_specs=[
            pl.BlockSpec((tm, d), row),
            pl.BlockSpec((1, 6, d), lambda b, t: (b, 0, 0)),
            _const_spec((1, 6, d)),
            _const_spec((1, d)),
            pl.BlockSpec(w.shape, lambda b, t: (0, 0), pipeline_mode=pl.Buffered(1)),
        ],
        out_specs=[pl.BlockSpec((tm, wd), row) for wd in _SEG_WIDTHS],
        out_shape=[jax.ShapeDtypeStruct((m, wd), BF16) for wd in _SEG_WIDTHS],
        compiler_params=_cparams("parallel", "parallel"),
        name="inproj",
    )(x, ml, mc, g, w)


_CONV_PAD = 16
_CONV_ROWS = 64


def _conv_body(z_ref, w_ref, b_ref, g_ref, bb_ref, o_ref, u_ref, c_ref, *, t_len, ctx_len):
    ch, pad, rc = CONV_CH, _CONV_PAD, _CONV_ROWS
    zeros = jnp.zeros((pad, ch), F32)
    u_ref[0:pad] = zeros
    u_ref[pad + ctx_len:2 * pad + ctx_len] = zeros
    u_ref[2 * pad + t_len:3 * pad + t_len] = zeros

    def u_row(r0):
        return pl.multiple_of(r0 + pad + jnp.where(r0 >= ctx_len, pad, 0), SUBLANES)

    def glu(i, carry):
        r0 = pl.multiple_of(i * rc, rc)
        z = z_ref[pl.ds(r0, rc), :].astype(F32)
        u_ref[pl.ds(u_row(r0), rc), :] = z[:, :ch] * _sigmoid(z[:, ch:])
        return carry

    lax.fori_loop(0, t_len // rc, glu, 0)

    win = rc + 2 * pad

    def chunk(i, carry):
        r0 = pl.multiple_of(i * rc, rc)
        base = pl.multiple_of(u_row(r0) - pad, SUBLANES)
        for cb in range(ch // LANES):
            ls = slice(cb * LANES, (cb + 1) * LANES)
            w = u_ref[pl.ds(base, win), ls]
            acc = jnp.zeros((rc, LANES), F32)
            for b in range(SUBLANES):
                wb = w if b == 0 else pltpu.roll(w, win - b, axis=0)
                for a in range(win // SUBLANES):
                    k = SUBLANES * a + b - (pad - CONV_WIDTH // 2)
                    if 0 <= k < CONV_WIDTH:
                        acc = acc + w_ref[k:k + 1, ls] * wb[SUBLANES * a:SUBLANES * a + rc]
            c_ref[:, ls] = acc + b_ref[:, ls]
        v = c_ref[...]
        mu = jnp.mean(v, axis=-1, keepdims=True)
        vc = v - mu
        var = jnp.mean(vc * vc, axis=-1, keepdims=True)
        y = vc * lax.rsqrt(var + NORM_EPS) * g_ref[...] + bb_ref[...]
        o_ref[pl.ds(r0, rc), :] = _silu(y).astype(BF16)
        return carry

    lax.fori_loop(0, t_len // rc, chunk, 0)


def _conv_call(zc, w, b, g, bb, *, nb, t_len, ctx_len):
    m = zc.shape[0]
    ch = CONV_CH
    return pl.pallas_call(
        functools.partial(_conv_body, t_len=t_len, ctx_len=ctx_len),
        grid=(nb,),
        in_specs=[
            pl.BlockSpec((t_len, 2 * ch), lambda i: (i, 0)),
            _const_spec((CONV_WIDTH, ch)),
            _const_spec((1, ch)),
            _const_spec((1, ch)),
            _const_spec((1, ch)),
        ],
        out_specs=pl.BlockSpec((t_len, ch), lambda i: (i, 0)),
        out_shape=jax.ShapeDtypeStruct((m, ch), BF16),
        scratch_shapes=[
            pltpu.VMEM((t_len + 3 * _CONV_PAD, ch), F32),
            pltpu.VMEM((_CONV_ROWS, ch), F32),
        ],
        compiler_params=_cparams("parallel"),
        name="conv",
    )(zc, w, b, g, bb)


_REC_ROWS = 128
_REC_PAD = SUBLANES


def _rec_body(z_ref, cw_ref, cb_ref, wa_ref, ba_ref, wi_ref, bi_ref, lam_ref, o_ref, xp_ref, hf_ref, win_ref, *, t_len,
              ctx_len):
    wd, ch, pad = REC_WIDTH, _REC_ROWS, _REC_PAD
    nch = t_len // ch
    nc_ctx = ctx_len // ch
    nblk = ch // SUBLANES
    zeros = jnp.zeros((pad, wd), F32)
    xp_ref[0:pad] = zeros
    xp_ref[pad + ctx_len:2 * pad + ctx_len] = zeros
    xp_ref[2 * pad + t_len:3 * pad + t_len] = zeros

    def xp_row(i):
        return pl.multiple_of(i * ch + pad + jnp.where(i >= nc_ctx, pad, 0), SUBLANES)

    def fill(i, carry):
        r0 = pl.multiple_of(i * ch, ch)
        xp_ref[pl.ds(xp_row(i), ch), :] = z_ref[pl.ds(r0, ch), 0:wd].astype(F32)
        return carry

    lax.fori_loop(0, nch, fill, 0)

    row8 = lax.broadcasted_iota(jnp.int32, (1, SUBLANES, 1), 1)

    def conv4(i, d):
        start = xp_row(i) - (pad if d == 0 else 0)
        win_ref[...] = xp_ref[pl.ds(pl.multiple_of(start, SUBLANES), ch + pad), :]
        lead = pad - (REC_CONV - 1) if d == 0 else 0
        acc = jnp.zeros((ch, wd), F32) + cb_ref[d]
        taps = cw_ref[d]
        for k in range(REC_CONV):
            acc = acc + taps[k:k + 1, :] * win_ref[lead + k:lead + k + ch, :]
        return acc

    def gates(xc, d):
        xb = xc.astype(BF16)
        r = _sigmoid(jnp.dot(xb, wa_ref[d], preferred_element_type=F32) + ba_ref[d])
        ig = _sigmoid(jnp.dot(xb, wi_ref[d], preferred_element_type=F32) + bi_ref[d])
        lam = lam_ref[d]
        softplus_neg = jnp.maximum(-lam, 0.0) + jnp.log1p(jnp.exp(-jnp.abs(lam)))
        log_a = -LRU_C * r * softplus_neg
        a = jnp.exp(log_a)
        bx = jnp.sqrt(-jnp.tanh(log_a) * (1.0 + a * a)) * (ig * xc)
        return a, bx

    def scan_chunk(a, b, carry, d):
        a = a.reshape(nblk, SUBLANES, wd)
        b = b.reshape(nblk, SUBLANES, wd)
        for s in (1, 2, 4):
            shift, keep = (s, row8 >= s) if d == 0 else (SUBLANES - s, row8 < SUBLANES - s)
            a_s, b_s = pltpu.roll(a, shift, axis=1), pltpu.roll(b, shift, axis=1)
            b = jnp.where(keep, a * b_s + b, b)
            a = jnp.where(keep, a * a_s, a)
        outs = [None] * nblk
        for j in (range(nblk) if d == 0 else reversed(range(nblk))):
            hj = b[j] + a[j] * carry
            outs[j] = hj
            carry = hj[SUBLANES - 1:SUBLANES] if d == 0 else hj[0:1]
        return jnp.concatenate(outs, axis=0), carry

    def fwd(i, carry):
        a, bx = gates(conv4(i, 0), 0)
        h, carry = scan_chunk(a, bx, carry, 0)
        hf_ref[pl.ds(pl.multiple_of(i * ch, ch), ch), :] = h
        return carry

    lax.fori_loop(0, nch, fwd, jnp.zeros((1, wd), F32))

    def bwd(j, carry):
        i = jnp.where(j < nc_ctx, nc_ctx - 1 - j, nch - 1 - (j - nc_ctx))
        a, bx = gates(conv4(i, 1), 1)
        h, carry = scan_chunk(a, bx, carry, 1)
        r0 = pl.multiple_of(i * ch, ch)
        gate = z_ref[pl.ds(r0, ch), wd:2 * wd].astype(F32)
        hsum = hf_ref[pl.ds(r0, ch), :] + h
        o_ref[pl.ds(r0, ch), :] = (hsum * jax.nn.gelu(gate)).astype(BF16)
        return carry

    lax.fori_loop(0, nch, bwd, jnp.zeros((1, wd), F32))


def _rec_call(zr, cw, cb, wa, ba, wi, bi, lam, *, nb, t_len, ctx_len):
    m = zr.shape[0]
    wd = REC_WIDTH
    return pl.pallas_call(
        functools.partial(_rec_body, t_len=t_len, ctx_len=ctx_len),
        grid=(nb,),
        in_specs=[
            pl.BlockSpec((t_len, 2 * wd), lambda i: (i, 0)),
            _const_spec((2, REC_CONV, wd)),
            _const_spec((2, 1, wd)),
            _const_spec((2, wd, wd)),
            _const_spec((2, 1, wd)),
            _const_spec((2, wd, wd)),
            _const_spec((2, 1, wd)),
            _const_spec((2, 1, wd)),
        ],
        out_specs=pl.BlockSpec((t_len, wd), lambda i: (i, 0)),
        out_shape=jax.ShapeDtypeStruct((m, wd), BF16),
        scratch_shapes=[
            pltpu.VMEM((t_len + 3 * _REC_PAD, wd), F32),
            pltpu.VMEM((t_len, wd), F32),
            pltpu.VMEM((_REC_ROWS + _REC_PAD, wd), F32),
        ],
        compiler_params=_cparams("parallel"),
        name="rglru",
    )(zr, cw, cb, wa, ba, wi, bi, lam)


def _qkv_body(zq_ref, zkv_ref, ta_ref, tb_ref, gqa_ref, gkva_ref, wq_ref, wkv_ref, gq_ref, gk_ref, q_ref, k_ref, v_ref):
    hp = HEAD_PAD
    zq = zq_ref[...].astype(F32)
    qa = zq * lax.rsqrt(jnp.mean(zq * zq, axis=-1, keepdims=True) + NORM_EPS) * gqa_ref[...]
    q = jnp.dot(qa.astype(BF16), wq_ref[...], preferred_element_type=F32)

    zkv = zkv_ref[...].astype(F32)
    lora = lax.broadcasted_iota(jnp.int32, (1, KV_IN_PAD), 1) < KV_LORA
    ms = jnp.sum(jnp.where(lora, zkv * zkv, 0.0), axis=-1, keepdims=True) * (1.0 / KV_LORA)
    lhs = jnp.where(lora, zkv * lax.rsqrt(ms + NORM_EPS) * gkva_ref[...], zkv)
    kv = jnp.dot(lhs.astype(BF16), wkv_ref[...], preferred_element_type=F32)
    nh = MLA_HEADS * hp
    lane_v = lax.broadcasted_iota(jnp.int32, (1, nh), 1) & (hp - 1)
    v_ref[...] = (kv[:, 2 * nh:] + jnp.where(lane_v == V_HEAD, 1.0, 0.0)).astype(BF16)

    ta, tb = ta_ref[...], tb_ref[...]
    scale = math.log2(math.e) / math.sqrt(QK_HEAD)
    q_a, q_b = ta * gq_ref[0:1, :] * scale, tb * gq_ref[1:2, :] * scale
    k_a, k_b = ta * gk_ref[0:1, :], tb * gk_ref[1:2, :]
    for h in range(MLA_HEADS):
        sl = slice(h * hp, (h + 1) * hp)
        sp = slice(nh + h * hp, nh + (h + 1) * hp)
        for src, fa, fb, ref in ((q, q_a, q_b, q_ref), (kv, k_a, k_b, k_ref)):
            t, tp = src[:, sl], src[:, sp]
            rs = lax.rsqrt(jnp.sum(t * t, axis=-1, keepdims=True) * (1.0 / QK_HEAD) + NORM_EPS)
            ref[:, sl] = (rs * (t * fa + tp * fb)).astype(BF16)


def _qkv_call(zq, zkv, ta, tb, gqa, gkva, wq, wkv, gq, gk, *, nb, t_len):
    m = zq.shape[0]
    tm = ROW_TILE
    nt = t_len // tm
    row = lambda b, t: (b * nt + t, 0)
    pos = lambda b, t: (t, 0)
    nq = MLA_HEADS * HEAD_PAD
    nv = nq
    return pl.pallas_call(
        _qkv_body,
        grid=(nb, nt),
        in_specs=[
            pl.BlockSpec((tm, Q_LORA), row),
            pl.BlockSpec((tm, KV_IN_PAD), row),
            pl.BlockSpec((tm, HEAD_PAD), pos),
            pl.BlockSpec((tm, HEAD_PAD), pos),
            _const_spec((1, Q_LORA)),
            _const_spec((1, KV_IN_PAD)),
            _const_spec(wq.shape),
            _const_spec(wkv.shape),
            _const_spec((2, HEAD_PAD)),
            _const_spec((2, HEAD_PAD)),
        ],
        out_specs=[pl.BlockSpec((tm, nq), row), pl.BlockSpec((tm, nq), row), pl.BlockSpec((tm, nv), row)],
        out_shape=[jax.ShapeDtypeStruct((m, nq), BF16), jax.ShapeDtypeStruct((m, nq), BF16),
                   jax.ShapeDtypeStruct((m, nv), BF16)],
        compiler_params=_cparams("parallel", "parallel"),
        name="qkv",
    )(zq, zkv, ta, tb, gqa, gkva, wq, wkv, gq, gk)


def _attn_body(q_ref, k_ref, v_ref, o_ref, *, t_len, ctx_len):
    hp = HEAD_PAD
    low = lax.broadcasted_iota(jnp.int32, (1, hp), 1) < V_HEAD

    def attend(nk, ahead):
        def scores(h):
            sl = slice(h * hp, (h + 1) * hp)
            return lax.dot_general(q_ref[:, sl], k_ref[0:nk, sl], (((1,), (1,)), ((), ())),
                                   preferred_element_type=F32)

        queue = [scores(h) for h in range(min(ahead, MLA_HEADS))]
        outs = []
        for h in range(MLA_HEADS):
            s = queue.pop(0)
            if h + ahead < MLA_HEADS:
                queue.append(scores(h + ahead))
            p = jnp.exp2(s - jnp.max(s, axis=-1, keepdims=True)).astype(BF16)
            pv = jnp.dot(p, v_ref[0:nk, h * hp:(h + 1) * hp], preferred_element_type=F32)
            outs.append(pv / pv[:, V_HEAD:V_HEAD + 1])
            if h % 2 == 1:
                both = jnp.where(low, outs[h - 1], pltpu.roll(outs[h], V_HEAD, axis=1))
                o_ref[:, (h // 2) * hp:(h // 2 + 1) * hp] = both.astype(BF16)

    is_ctx_tile = pl.program_id(1) * ROW_TILE < ctx_len

    @pl.when(is_ctx_tile)
    def _():
        attend(ctx_len, ahead=MLA_HEADS)

    @pl.when(jnp.logical_not(is_ctx_tile))
    def _():
        attend(t_len, ahead=2)


def _attn_call(q, k, v, *, nb, t_len, ctx_len):
    m = q.shape[0]
    tq = ROW_TILE
    nt = t_len // tq
    nq = MLA_HEADS * HEAD_PAD
    nv = MLA_HEADS * V_HEAD
    return pl.pallas_call(
        functools.partial(_attn_body, t_len=t_len, ctx_len=ctx_len),
        grid=(nb, nt),
        in_specs=[
            pl.BlockSpec((tq, nq), lambda b, t: (b * nt + t, 0)),
            pl.BlockSpec((t_len, nq), lambda b, t: (b, 0)),
            pl.BlockSpec((t_len, nq), lambda b, t: (b, 0)),
        ],
        out_specs=pl.BlockSpec((tq, nv), lambda b, t: (b * nt + t, 0)),
        out_shape=jax.ShapeDtypeStruct((m, nv), BF16),
        compiler_params=_cparams("parallel", "arbitrary"),
        name="attn",
    )(q, k, v)


_ROUTER_LOW_LANE = 64


def _merge_body(ca_ref, ao_ref, rr_ref, zg_ref, x_ref, ml_ref, mc_ref, bg_ref, g2_ref, wc_ref, wm_ref, wr_ref,
                wo_ref, *rest, tm, ctx_len, n_exp):
    moe = n_exp > 0
    d = x_ref.shape[-1]
    is_ctx = _is_ctx(pl.program_id(1), tm, ctx_len)
    merged = jnp.zeros((tm, d), F32)
    for j, (src, w) in enumerate(((ca_ref, wc_ref), (ao_ref, wm_ref), (rr_ref, wr_ref))):
        gate = _sigmoid(zg_ref[:, j * d:(j + 1) * d].astype(F32) + bg_ref[:, j * d:(j + 1) * d])
        merged = merged + gate * jnp.dot(src[...], w[...], preferred_element_type=F32)
    y = jnp.dot(merged.astype(BF16), wo_ref[...], preferred_element_type=F32)
    x = x_ref[...] + _mod_row(ml_ref, mc_ref, 2, is_ctx) * y
    h2 = _rms_mod(x, g2_ref[...], _mod_row(ml_ref, mc_ref, 3, is_ctx), _mod_row(ml_ref, mc_ref, 4, is_ctx))
    if not moe:
        xo_ref, h_ref = rest
        xo_ref[...] = x
        h_ref[...] = h2.astype(BF16)
        return
    wrt_ref, xo_ref, h_ref, ei_ref, ew_ref = rest
    xo_ref[...] = x
    for s in range(d // LANES):
        h_ref[pl.ds(s, tm, stride=SUBLANES), :] = h2[:, s * LANES:(s + 1) * LANES]
    h_hi = h2.astype(BF16)
    h_lo = (h2 - h_hi.astype(F32)).astype(BF16)
    part = (jnp.dot(h_hi, wrt_ref[...], preferred_element_type=F32)
            + jnp.dot(h_lo, wrt_ref[...], preferred_element_type=F32))
    logits = part + pltpu.roll(part, LANES - _ROUTER_LOW_LANE, axis=1)
    lane = lax.broadcasted_iota(jnp.int32, (tm, LANES), 1).astype(F32)
    logits = jnp.where(lane < n_exp, logits, -jnp.inf)
    m1 = jnp.max(logits, axis=-1, keepdims=True)
    i1 = jnp.min(jnp.where(logits == m1, lane, float(LANES)), axis=-1, keepdims=True)
    rest_l = jnp.where(lane == i1, -jnp.inf, logits)
    m2 = jnp.max(rest_l, axis=-1, keepdims=True)
    i2 = jnp.min(jnp.where(rest_l == m2, lane, float(LANES)), axis=-1, keepdims=True)
    e2 = jnp.exp(m2 - m1)
    w1 = 1.0 / (1.0 + e2)
    w2 = e2 / (1.0 + e2)
    ei_ref[...] = jnp.where(lane == 0.0, i1, jnp.where(lane == 1.0, i2, 0.0)).astype(jnp.int32)
    ew_ref[...] = jnp.where(lane == 0.0, w1, jnp.where(lane == 1.0, w2, 0.0))


def _merge_call(ca, ao, rr, zg, x, ml, mc, bg, g2, wc, wm, wr, wo, w_router, *, nb, t_len, ctx_len):
    m, d = x.shape
    tm = ROW_TILE
    nt = t_len // tm
    row = lambda b, t: (b * nt + t, 0)
    moe = w_router is not None
    n_exp = w_router.shape[1] if moe else 0
    wrt = None
    if moe:
        assert n_exp <= _ROUTER_LOW_LANE
        w_hi = w_router.astype(BF16)
        w_lo = (w_router - w_hi.astype(F32)).astype(BF16)
        wrt = jnp.zeros((d, LANES), BF16).at[:, :n_exp].set(w_hi).at[:, _ROUTER_LOW_LANE:_ROUTER_LOW_LANE + n_exp].set(w_lo)
    in_specs = [
        pl.BlockSpec((tm, ca.shape[1]), row),
        pl.BlockSpec((tm, ao.shape[1]), row),
        pl.BlockSpec((tm, rr.shape[1]), row),
        pl.BlockSpec((tm, zg.shape[1]), row),
        pl.BlockSpec((tm, d), row),
        pl.BlockSpec((1, 6, d), lambda b, t: (b, 0, 0)),
        _const_spec((1, 6, d)),
        _const_spec(bg.shape),
        _const_spec(g2.shape),
        _const_spec(wc.shape),
        _const_spec(wm.shape),
        _const_spec(wr.shape),
        _const_spec(wo.shape),
    ]
    args = [ca, ao, rr, zg, x, ml, mc, bg, g2, wc, wm, wr, wo]
    out_specs = [pl.BlockSpec((tm, d), row)]
    out_shape = [jax.ShapeDtypeStruct((m, d), F32)]
    if moe:
        in_specs.append(_const_spec(wrt.shape))
        args.append(wrt)
        out_specs += [pl.BlockSpec((tm * SUBLANES, LANES), row), pl.BlockSpec((tm, LANES), row),
                      pl.BlockSpec((tm, LANES), row)]
        out_shape += [jax.ShapeDtypeStruct((m * SUBLANES, LANES), F32), jax.ShapeDtypeStruct((m, LANES), jnp.int32),
                      jax.ShapeDtypeStruct((m, LANES), F32)]
    else:
        out_specs.append(pl.BlockSpec((tm, d), row))
        out_shape.append(jax.ShapeDtypeStruct((m, d), BF16))
    return pl.pallas_call(
        functools.partial(_merge_body, tm=tm, ctx_len=ctx_len, n_exp=n_exp),
        grid=(nb, nt),
        in_specs=in_specs,
        out_specs=out_specs,
        out_shape=out_shape,
        input_output_aliases={4: 0},
        compiler_params=_cparams("parallel", "parallel"),
        name="merge_moe" if moe else "merge",
    )(*args)


def _ffn_body(h_ref, x_ref, ml_ref, mc_ref, wg_ref, wu_ref, wd_ref, o_ref, acc_ref, *, tm, ctx_len):
    acc_ref[...] = jnp.zeros_like(acc_ref)

    def chunk(f, carry):
        h = h_ref[...]
        g = jnp.dot(h, wg_ref[f], preferred_element_type=F32)
        u = jnp.dot(h, wu_ref[f], preferred_element_type=F32)
        acc_ref[...] += jnp.dot((_silu(g) * u).astype(BF16), wd_ref[f], preferred_element_type=F32)
        return carry

    lax.fori_loop(0, wg_ref.shape[0], chunk, 0)
    is_ctx = _is_ctx(pl.program_id(1), tm, ctx_len)
    o_ref[...] = x_ref[...] + _mod_row(ml_ref, mc_ref, 5, is_ctx) * acc_ref[...]


def _ffn_call(h, x, ml, mc, wg, wu, wd, *, nb, t_len, ctx_len):
    m, d = x.shape
    tm = FFN_ROWS if t_len % FFN_ROWS == 0 else ROW_TILE
    nt = t_len // tm
    row = lambda b, t: (b * nt + t, 0)
    resident = lambda w: pl.BlockSpec(w.shape, lambda b, t: (0, 0, 0), pipeline_mode=pl.Buffered(1))
    return pl.pallas_call(
        functools.partial(_ffn_body, tm=tm, ctx_len=ctx_len),
        grid=(nb, nt),
        in_specs=[
            pl.BlockSpec((tm, d), row),
            pl.BlockSpec((tm, d), row),
            pl.BlockSpec((1, 6, d), lambda b, t: (b, 0, 0)),
            _const_spec((1, 6, d)),
            resident(wg),
            resident(wu),
            resident(wd),
        ],
        out_specs=pl.BlockSpec((tm, d), row),
        out_shape=jax.ShapeDtypeStruct((m, d), F32),
        scratch_shapes=[pltpu.VMEM((tm, d), F32)],
        input_output_aliases={1: 0},
        compiler_params=_cparams("parallel", "parallel"),
        name="ffn",
    )(h, x, ml, mc, wg, wu, wd)


def _gather_tiles(idx_ref, src_hbm, dst_ref, slot, sem, n):
    def issue(r, carry):
        src = pl.multiple_of(idx_ref[0, 0, r] * SUBLANES, SUBLANES)
        pltpu.make_async_copy(src_hbm.at[pl.ds(src, SUBLANES), :],
                              dst_ref.at[slot, pl.ds(pl.multiple_of(r * SUBLANES, SUBLANES), SUBLANES), :],
                              sem.at[slot]).start()
        return carry

    lax.fori_loop(0, n, issue, 0, unroll=8)


def _wait_tiles(src_hbm, dst_ref, slot, sem, n):
    pltpu.make_async_copy(src_hbm.at[pl.ds(0, n * SUBLANES), :], dst_ref.at[slot], sem.at[slot]).wait()


def _moe_body(te_ref, nu_ref, st_ref, stn_ref, h_hbm, wg_ref, wu_ref, wd_ref, y_ref, xt_ref, xb_ref, acc_ref, sem, *,
              tm):
    i = pl.program_id(0)
    nf = wg_ref.shape[1]
    d = xb_ref.shape[1]
    n_used = nu_ref[0]
    valid = i < n_used
    slot = i & 1

    @pl.when(i == 0)
    def _():
        _gather_tiles(st_ref, h_hbm, xt_ref, 0, sem, tm)

    @pl.when(i <= n_used)
    def _():
        _wait_tiles(h_hbm, xt_ref, slot, sem, tm)

    @pl.when(valid)
    def _():
        for s in range(d // LANES):
            xb_ref[:, s * LANES:(s + 1) * LANES] = xt_ref[slot, pl.ds(s, tm, stride=SUBLANES), :].astype(BF16)
        acc_ref[...] = jnp.zeros_like(acc_ref)
        per_step = tm // nf

        def chunk(f, carry):
            for r in range(per_step):
                rr = f * per_step + r
                src = pl.multiple_of(stn_ref[0, 0, rr] * SUBLANES, SUBLANES)
                pltpu.make_async_copy(
                    h_hbm.at[pl.ds(src, SUBLANES), :],
                    xt_ref.at[1 - slot, pl.ds(pl.multiple_of(rr * SUBLANES, SUBLANES), SUBLANES), :],
                    sem.at[1 - slot]).start()
            x = xb_ref[...]
            g = jnp.dot(x, wg_ref[0, f], preferred_element_type=F32)
            u = jnp.dot(x, wu_ref[0, f], preferred_element_type=F32)
            acc_ref[...] += jnp.dot((_silu(g) * u).astype(BF16), wd_ref[0, f], preferred_element_type=F32)
            return carry

        lax.fori_loop(0, nf, chunk, 0)
        for s in range(d // LANES):
            y_ref[pl.ds(s, tm, stride=SUBLANES), :] = acc_ref[:, s * LANES:(s + 1) * LANES]

    @pl.when(jnp.logical_not(valid))
    def _():
        y_ref[...] = jnp.zeros_like(y_ref)


def _moe_call(tile_expert, n_used, slot_token, h_tiles, wg, wu, wd):
    n_exp, nf, d, tf = wg.shape
    tm = MOE_TILE
    assert tm % nf == 0
    n_tiles = slot_token.shape[0]
    expert = lambda w: pl.BlockSpec((1,) + w.shape[1:], lambda i, te, nu: (te[i], 0, 0, 0),
                                    pipeline_mode=pl.Buffered(1))
    grid_spec = pltpu.PrefetchScalarGridSpec(
        num_scalar_prefetch=2,
        grid=(n_tiles,),
        in_specs=[
            pl.BlockSpec((1, 1, tm), lambda i, te, nu: (i, 0, 0), memory_space=pltpu.SMEM),
            pl.BlockSpec((1, 1, tm), lambda i, te, nu: (jnp.minimum(i + 1, n_tiles - 1), 0, 0),
                         memory_space=pltpu.SMEM),
            pl.BlockSpec(memory_space=pl.ANY),
            expert(wg),
            expert(wu),
            expert(wd),
        ],
        out_specs=pl.BlockSpec((tm * SUBLANES, LANES), lambda i, te, nu: (i, 0)),
        scratch_shapes=[
            pltpu.VMEM((2, tm * SUBLANES, LANES), F32),
            pltpu.VMEM((tm, d), BF16),
            pltpu.VMEM((tm, d), F32),
            pltpu.SemaphoreType.DMA((2,)),
        ],
    )
    return pl.pallas_call(
        functools.partial(_moe_body, tm=tm),
        grid_spec=grid_spec,
        out_shape=jax.ShapeDtypeStruct((n_tiles * tm * SUBLANES, LANES), F32),
        compiler_params=pltpu.CompilerParams(dimension_semantics=("arbitrary",), vmem_limit_bytes=MOE_VMEM_LIMIT),
        name="moe",
    )(tile_expert, n_used, slot_token, slot_token, h_tiles, wg, wu, wd)


def _combine_body(p1_ref, p2_ref, p1n_ref, p2n_ref, y_hbm, ew_ref, x_ref, ml_ref, mc_ref, o_ref, y1_ref, y2_ref, sem,
                  *, tm, ctx_len, tile_of, nt):
    n = pl.program_id(0)
    d = x_ref.shape[1]
    slot = n & 1

    @pl.when(n == 0)
    def _():
        _gather_tiles(p1_ref, y_hbm, y1_ref, 0, sem.at[0], tm)
        _gather_tiles(p2_ref, y_hbm, y2_ref, 0, sem.at[1], tm)

    @pl.when(n + 1 < pl.num_programs(0))
    def _():
        _gather_tiles(p1n_ref, y_hbm, y1_ref, 1 - slot, sem.at[0], tm)
        _gather_tiles(p2n_ref, y_hbm, y2_ref, 1 - slot, sem.at[1], tm)

    _wait_tiles(y_hbm, y1_ref, slot, sem.at[0], tm)
    _wait_tiles(y_hbm, y2_ref, slot, sem.at[1], tm)
    is_ctx = _is_ctx(tile_of(n) % nt, tm, ctx_len)
    w1, w2 = ew_ref[:, 0:1], ew_ref[:, 1:2]
    ga = _mod_row(ml_ref, mc_ref, 5, is_ctx)
    for s in range(d // LANES):
        ls = slice(s * LANES, (s + 1) * LANES)
        f = (w1 * y1_ref[slot, pl.ds(s, tm, stride=SUBLANES), :]
             + w2 * y2_ref[slot, pl.ds(s, tm, stride=SUBLANES), :])
        o_ref[:, ls] = x_ref[:, ls] + ga[:, ls] * f


def _combine_call(pos1, pos2, y_tiles, ew, x, ml, mc, *, nb, t_len, ctx_len, latent_only):
    m, d = x.shape
    tm = ROW_TILE
    nt = t_len // tm
    nc = ctx_len // tm if latent_only else 0
    per = nt - nc
    n_steps = nb * per

    def tile_of(n):
        return (n // per) * nt + nc + n % per

    cur = lambda n: (n, 0, 0)
    nxt = lambda n: (jnp.minimum(n + 1, n_steps - 1), 0, 0)
    row = lambda n: (tile_of(n), 0)
    return pl.pallas_call(
        functools.partial(_combine_body, tm=tm, ctx_len=ctx_len, tile_of=tile_of, nt=nt),
        grid=(n_steps,),
        in_specs=[
            pl.BlockSpec((1, 1, tm), cur, memory_space=pltpu.SMEM),
            pl.BlockSpec((1, 1, tm), cur, memory_space=pltpu.SMEM),
            pl.BlockSpec((1, 1, tm), nxt, memory_space=pltpu.SMEM),
            pl.BlockSpec((1, 1, tm), nxt, memory_space=pltpu.SMEM),
            pl.BlockSpec(memory_space=pl.ANY),
            pl.BlockSpec((tm, LANES), row),
            pl.BlockSpec((tm, d), row),
            pl.BlockSpec((1, 6, d), lambda n: (n // per, 0, 0)),
            _const_spec((1, 6, d)),
        ],
        out_specs=pl.BlockSpec((tm, d), lambda n: (n, 0)),
        out_shape=jax.ShapeDtypeStruct((n_steps * tm, d), F32),
        scratch_shapes=[
            pltpu.VMEM((2, tm * SUBLANES, LANES), F32),
            pltpu.VMEM((2, tm * SUBLANES, LANES), F32),
            pltpu.SemaphoreType.DMA((2, 2)),
        ],
        input_output_aliases={} if latent_only else {6: 0},
        compiler_params=_cparams("arbitrary"),
        name="moe_combine",
    )(pos1, pos2, pos1, pos2, y_tiles, ew, x, ml, mc)


def _route_tables(eidx, token_rows, n_exp, tile):
    m = eidx.shape[0]
    e = eidx.reshape(-1)
    onehot = (e[:, None] == jnp.arange(n_exp, dtype=jnp.int32)[None, :]).astype(jnp.int32)
    csum = jnp.cumsum(onehot, axis=0)
    rank = jnp.sum(csum * onehot, axis=1) - 1
    tiles_per = (csum[-1] + tile - 1) // tile
    tile_end = jnp.cumsum(tiles_per)
    tile_start = tile_end - tiles_per
    pos = jnp.sum(onehot * tile_start[None, :], axis=1) * tile + rank
    n_tiles = -(-(TOP_K * m) // tile) + n_exp
    tidx = jnp.arange(n_tiles, dtype=jnp.int32)
    n_used = tile_end[-1].astype(jnp.int32)
    texp = jnp.sum((tidx[:, None] >= tile_end[None, :]).astype(jnp.int32), axis=1)
    last = jnp.sum((n_used - 1 >= tile_end).astype(jnp.int32))
    texp = jnp.where(tidx < n_used, texp, last).astype(jnp.int32)
    slot_token = jnp.zeros((n_tiles * tile,), jnp.int32).at[pos].set(jnp.repeat(token_rows, TOP_K),
                                                                   unique_indices=True)
    pos = pos.reshape(m, TOP_K).astype(jnp.int32)
    return texp, n_used.reshape(1), slot_token.reshape(n_tiles, 1, tile), pos[:, 0], pos[:, 1]


def _rope_tables(seq, ctx_len):
    rows = seq // GRID_W
    row = jnp.repeat(jnp.arange(rows, dtype=jnp.int32), GRID_W).astype(F32)
    col = jnp.tile(jnp.arange(GRID_W, dtype=jnp.int32), rows).astype(F32)
    half = QK_ROPE // 2
    freqs = ROPE_BASE ** (-jnp.arange(0, half, 2, dtype=F32) / half)
    ar, ac = row[:, None] * freqs, col[:, None] * freqs
    cos = jnp.concatenate([jnp.cos(ar), jnp.cos(ar), jnp.cos(ac), jnp.cos(ac)], axis=1)
    sin = jnp.concatenate([-jnp.sin(ar), jnp.sin(ar), -jnp.sin(ac), jnp.sin(ac)], axis=1)
    ones = jnp.ones((seq, QK_NOPE), F32)
    ta = jnp.concatenate([ones, cos, jnp.ones((seq, HEAD_PAD - QK_HEAD), F32)], axis=1)
    tb = jnp.concatenate([0 * ones, sin, jnp.zeros((seq, HEAD_PAD - QK_HEAD), F32)], axis=1)
    ta = jnp.concatenate([jnp.ones((ctx_len, HEAD_PAD), F32), ta], axis=0)
    tb = jnp.concatenate([jnp.zeros((ctx_len, HEAD_PAD), F32), tb], axis=0)
    return ta, tb


def _rope_partner():
    q = QK_ROPE // 4
    return jnp.array(list(range(q, 2 * q)) + list(range(0, q)) + list(range(3 * q, 4 * q)) + list(range(2 * q, 3 * q)),
                     dtype=jnp.int32)


def _head_gains(g):
    perm = _rope_partner()
    zeros = jnp.zeros((HEAD_PAD - QK_HEAD,), F32)
    g_a = jnp.concatenate([g, zeros])
    g_b = jnp.concatenate([jnp.zeros((QK_NOPE,), F32), g[QK_NOPE:][perm], zeros])
    return jnp.stack([g_a, g_b])


def _pack_wq(w_q_b):
    perm = _rope_partner()
    w = w_q_b.reshape(Q_LORA, MLA_HEADS, QK_HEAD)
    tail = jnp.zeros((Q_LORA, MLA_HEADS, HEAD_PAD - QK_HEAD), F32)
    main = jnp.concatenate([w, tail], axis=-1)
    partner = jnp.concatenate([jnp.zeros((Q_LORA, MLA_HEADS, QK_NOPE), F32), w[:, :, QK_NOPE:][:, :, perm], tail], axis=-1)
    nh = MLA_HEADS * HEAD_PAD
    return jnp.concatenate([main.reshape(Q_LORA, nh), partner.reshape(Q_LORA, nh)], axis=1).astype(BF16)


def _pack_wkv(w_kv_b):
    perm = _rope_partner()
    w = w_kv_b.reshape(KV_LORA, MLA_HEADS, QK_NOPE + V_HEAD)
    eye = jnp.eye(QK_ROPE, dtype=F32)
    left = jnp.zeros((QK_ROPE, QK_NOPE), F32)
    right = jnp.zeros((QK_ROPE, HEAD_PAD - QK_HEAD), F32)
    nh = MLA_HEADS * HEAD_PAD

    def per_head(rope_rows):
        return jnp.broadcast_to(rope_rows[:, None, :], (QK_ROPE, MLA_HEADS, HEAD_PAD)).reshape(QK_ROPE, nh)

    k_top = jnp.concatenate([w[:, :, :QK_NOPE], jnp.zeros((KV_LORA, MLA_HEADS, HEAD_PAD - QK_NOPE), F32)], axis=-1)
    main = jnp.concatenate([k_top.reshape(KV_LORA, nh), per_head(jnp.concatenate([left, eye, right], axis=1))], axis=0)
    partner = jnp.concatenate([jnp.zeros((KV_LORA, nh), F32),
                               per_head(jnp.concatenate([left, eye[:, perm], right], axis=1))], axis=0)
    v_top = jnp.concatenate([w[:, :, QK_NOPE:], jnp.zeros((KV_LORA, MLA_HEADS, HEAD_PAD - V_HEAD), F32)], axis=-1)
    v_cols = jnp.concatenate([v_top.reshape(KV_LORA, nh), jnp.zeros((QK_ROPE, nh), F32)], axis=0)
    full = jnp.concatenate([main, partner, v_cols], axis=1)
    return jnp.pad(full, ((0, KV_IN_PAD - KV_LORA - QK_ROPE), (0, 0))).astype(BF16)


def _pack_w_in(w_in):
    o1 = 2 * CONV_CH
    o2 = o1 + Q_LORA
    o3 = o2 + KV_LORA + QK_ROPE
    kv = jnp.pad(w_in[:, o2:o3], ((0, 0), (0, KV_IN_PAD - KV_LORA - QK_ROPE)))
    return jnp.concatenate([w_in[:, :o2], kv, w_in[:, o3:]], axis=1).astype(BF16)


def _chunk_cols(w):
    *lead, d, dff = w.shape
    w = w.reshape(*lead, d, dff // FFN_CHUNK, FFN_CHUNK)
    return jnp.swapaxes(w, -3, -2).astype(BF16)


def _chunk_rows(w):
    *lead, dff, d = w.shape
    return w.reshape(*lead, dff // FFN_CHUNK, FFN_CHUNK, d).astype(BF16)


def _block_diag(w):
    nd, nblk, bw, _ = w.shape
    eye = jnp.eye(nblk, dtype=w.dtype)
    return jnp.einsum("dgij,gh->dgihj", w, eye).reshape(nd, nblk * bw, nblk * bw).astype(BF16)


def kernel(x, c, ctx, c_ctx, w_ada, b_ada, g_norm1, g_norm2, w_in, b_gate, conv_w, conv_b, conv_ln_g, conv_ln_b, w_o_conv, g_q_a, w_q_b, g_kv_a, w_kv_b, g_qn, g_kn, w_o_mla, rec_conv_w, rec_conv_b, w_ra, b_ra, w_ri, b_ri, lru_lambda, w_o_rec, w_out, w_ff_gate, w_ff_up, w_ff_down, w_router, w_e_gate, w_e_up, w_e_down):
    nb, seq, d = x.shape
    ctx_len = ctx.shape[1]
    depth = w_ada.shape[0]
    t_len = ctx_len + seq
    m = nb * t_len
    n_exp = w_router.shape[-1]
    assert d == 1024 and ctx_len % ROW_TILE == 0 and seq % ROW_TILE == 0 and seq % GRID_W == 0
    dims =dict(nb=nb, t_len=t_len, ctx_len=ctx_len)

    r_pad = -(-(nb + 1) // SUBLANES) * SUBLANES
    cpad = jnp.zeros((r_pad, d), F32).at[:nb].set(c).at[nb].set(c_ctx)
    mods = _ada_call(cpad, w_ada, b_ada).reshape(depth, r_pad, 6, d)

    ta, tb = _rope_tables(seq, ctx_len)
    xs = jnp.concatenate([ctx, x], axis=1).reshape(m, d)

    for i in range(depth):
        moe = i % 2 == 1
        j = i // 2
        ml, mc = mods[i, :nb], mods[i, nb:nb + 1]
        zc, zq, zkv, zr, zg = _inproj_call(xs, ml, mc, g_norm1[i][None], _pack_w_in(w_in[i]), **dims)
        ca = _conv_call(zc, conv_w[i], conv_b[i][None], conv_ln_g[i][None], conv_ln_b[i][None], **dims)
        rr = _rec_call(zr, rec_conv_w[i], rec_conv_b[i][:, None], _block_diag(w_ra[i]), b_ra[i][:, None],
                       _block_diag(w_ri[i]), b_ri[i][:, None], lru_lambda[i][:, None], **dims)
        gkva = jnp.concatenate([g_kv_a[i], jnp.ones((KV_IN_PAD - KV_LORA,), F32)])[None]
        q, k, v = _qkv_call(zq, zkv, ta, tb, g_q_a[i][None], gkva, _pack_wq(w_q_b[i]), _pack_wkv(w_kv_b[i]),
                            _head_gains(g_qn[i]), _head_gains(g_kn[i]), nb=nb, t_len=t_len)
        ao = _attn_call(q, k, v, **dims)
        wrt = w_router[j] if moe else None
        outs = _merge_call(ca, ao, rr, zg, xs, ml, mc, b_gate[i][None], g_norm2[i][None], w_o_conv[i].astype(BF16),
                           w_o_mla[i].astype(BF16), w_o_rec[i].astype(BF16), w_out[i].astype(BF16), wrt, **dims)
        if not moe:
            xs, h2 = outs
            xs = _ffn_call(h2, xs, ml, mc, _chunk_cols(w_ff_gate[j]), _chunk_cols(w_ff_up[j]),
                           _chunk_rows(w_ff_down[j]), **dims)
            continue
        xs, h_tiles, eidx, ew = outs
        latent_only = i == depth - 1
        rows = jnp.arange(m, dtype=jnp.int32).reshape(nb, t_len)
        experts = eidx[:, :TOP_K].reshape(nb, t_len, TOP_K)
        if latent_only:
            rows, experts = rows[:, ctx_len:], experts[:, ctx_len:]
        rows = rows.reshape(-1)
        texp, n_used, slot_token, pos1, pos2 = _route_tables(experts.reshape(-1, TOP_K), rows, n_exp, MOE_TILE)
        y_tiles = _moe_call(texp, n_used, slot_token, h_tiles, _chunk_cols(w_e_gate[j]), _chunk_cols(w_e_up[j]),
                            _chunk_rows(w_e_down[j]))
        nrt = rows.shape[0] // ROW_TILE
        xs = _combine_call(pos1.reshape(nrt, 1, ROW_TILE), pos2.reshape(nrt, 1, ROW_TILE), y_tiles, ew, xs, ml, mc,
                           latent_only=latent_only, **dims)
        if latent_only:
            return xs.reshape(nb, seq, d)
    return xs.reshape(nb, t_len, d)[:, ctx_len:]
```

```python
import functools
import math

import jax
import jax.numpy as jnp
from jax import lax
from jax.experimental import pallas as pl
from jax.experimental.pallas import tpu as pltpu

F32 = jnp.float32
BF16 = jnp.bfloat16

NORM_EPS = 1e-6
GRID_W = 64
CONV_CH = 512
CONV_WIDTH = 31
MLA_HEADS = 8
QK_NOPE = 64
QK_ROPE = 32
V_HEAD = 64
Q_LORA = 768
KV_LORA = 256
QK_HEAD = QK_NOPE + QK_ROPE
REC_WIDTH = 512
REC_CONV = 4
LRU_C = 8.0
N_BRANCH = 3
TOP_K = 2
ROPE_BASE = 10000.0

LANES = 128
SUBLANES = 8
HEAD_PAD = LANES
KV_IN_PAD = 384
ROW_TILE = 256
FFN_CHUNK = 512
FFN_ROWS = 768
ADA_COLS = 1536
MOE_TILE = 896
MOE_GATHER_CHUNKS = 4
VMEM_LIMIT = 56 * 1024 * 1024
MOE_VMEM_LIMIT = 60 * 1024 * 1024


def _cparams(*sem):
    return pltpu.CompilerParams(dimension_semantics=sem, vmem_limit_bytes=VMEM_LIMIT)


def _const_spec(shape):
    nd = len(shape)
    return pl.BlockSpec(shape, lambda *_: (0,) * nd)


def _sigmoid(x):
    return 0.5 * jnp.tanh(0.5 * x) + 0.5


def _silu(x):
    hx = 0.5 * x
    return hx * jnp.tanh(hx) + hx


def _mod_row(ml_ref, mc_ref, k, is_ctx):
    return jnp.where(is_ctx, mc_ref[0, k:k + 1, :], ml_ref[0, k:k + 1, :])


def _is_ctx(t, tm, ctx_len):
    row = t * tm + lax.broadcasted_iota(jnp.int32, (tm, 1), 0)
    return row < ctx_len


def _rms_mod(x, g, shift, scale):
    y = x * lax.rsqrt(jnp.mean(x * x, axis=-1, keepdims=True) + NORM_EPS) * g
    return y * (1.0 + scale) + shift


def _ada_body(c_ref, w_ref, b_ref, o_ref):
    c = c_ref[...]
    s = _silu(c).astype(BF16)
    o_ref[0] = jnp.dot(s, w_ref[0].astype(BF16), preferred_element_type=F32) + b_ref[0]


def _ada_call(cpad, w_ada, b_ada):
    depth, d, n = w_ada.shape
    r = cpad.shape[0]
    tn = ADA_COLS
    return pl.pallas_call(
        _ada_body,
        grid=(depth, n // tn),
        in_specs=[
            pl.BlockSpec((r, d), lambda l, j: (0, 0)),
            pl.BlockSpec((1, d, tn), lambda l, j: (l, 0, j)),
            pl.BlockSpec((1, 1, tn), lambda l, j: (l, 0, j)),
        ],
        out_specs=pl.BlockSpec((1, r, tn), lambda l, j: (l, 0, j)),
        out_shape=jax.ShapeDtypeStruct((depth, r, n), F32),
        compiler_params=_cparams("arbitrary", "arbitrary"),
        name="ada",
    )(cpad, w_ada, b_ada.reshape(depth, 1, n))


_SEG_WIDTHS = (2 * CONV_CH, Q_LORA, KV_IN_PAD, 2 * REC_WIDTH, N_BRANCH * 1024)
_DOT_COLS = 512


def _inproj_body(x_ref, ml_ref, mc_ref, g_ref, w_ref, *out_refs, tm, ctx_len):
    is_ctx = _is_ctx(pl.program_id(1), tm, ctx_len)
    h = _rms_mod(x_ref[...], g_ref[...], _mod_row(ml_ref, mc_ref, 0, is_ctx), _mod_row(ml_ref, mc_ref, 1, is_ctx))
    hb = h.astype(BF16)
    c0 = 0
    for ref, width in zip(out_refs, _SEG_WIDTHS):
        for j in range(0, width, _DOT_COLS):
            cw = min(_DOT_COLS, width - j)
            ref[:, j:j + cw] = jnp.dot(hb, w_ref[:, c0 + j:c0 + j + cw], preferred_element_type=F32).astype(BF16)
        c0 += width


def _inproj_call(x, ml, mc, g, w, *, nb, t_len, ctx_len):
    m, d = x.shape
    tm = ROW_TILE
    nt = t_len // tm
    row = lambda b, t: (b * nt + t, 0)
    return pl.pallas_call(
        functools.partial(_inproj_body, tm=tm, ctx_len=ctx_len),
        grid=(nb, nt),
        in_specs=[
            pl.BlockSpec((tm, d), row),
            pl.BlockSpec((1, 6, d), lambda b, t: (b, 0, 0)),
            _const_spec((1, 6, d)),
            _const_spec((1, d)),
            pl.BlockSpec(w.shape, lambda b, t: (0, 0), pipeline_mode=pl.Buffered(1)),
        ],
        out_specs=[pl.BlockSpec((tm, wd), row) for wd in _SEG_WIDTHS],
        out_shape=[jax.ShapeDtypeStruct((m, wd), BF16) for wd in _SEG_WIDTHS],
        compiler_params=_cparams("parallel", "parallel"),
        name="inproj",
    )(x, ml, mc, g, w)


_CONV_PAD = 16
_CONV_ROWS = 64


def _conv_body(z_ref, w_ref, b_ref, g_ref, bb_ref, o_ref, u_ref, c_ref, *, t_len, ctx_len):
    ch, pad, rc = CONV_CH, _CONV_PAD, _CONV_ROWS
    zeros = jnp.zeros((pad, ch), F32)
    u_ref[0:pad] = zeros
    u_ref[pad + ctx_len:2 * pad + ctx_len] = zeros
    u_ref[2 * pad + t_len:3 * pad + t_len] = zeros

    def u_row(r0):
        return pl.multiple_of(r0 + pad + jnp.where(r0 >= ctx_len, pad, 0), SUBLANES)

    def glu(i, carry):
        r0 = pl.multiple_of(i * rc, rc)
        z = z_ref[pl.ds(r0, rc), :].astype(F32)
        u_ref[pl.ds(u_row(r0), rc), :] = z[:, :ch] * _sigmoid(z[:, ch:])
        return carry

    lax.fori_loop(0, t_len // rc, glu, 0)

    win = rc + 2 * pad

    def chunk(i, carry):
        r0 = pl.multiple_of(i * rc, rc)
        base = pl.multiple_of(u_row(r0) - pad, SUBLANES)
        for cb in range(ch // LANES):
            ls = slice(cb * LANES, (cb + 1) * LANES)
            w = u_ref[pl.ds(base, win), ls]
            acc = jnp.zeros((rc, LANES), F32)
            for b in range(SUBLANES):
                wb = w if b == 0 else pltpu.roll(w, win - b, axis=0)
                for a in range(win // SUBLANES):
                    k = SUBLANES * a + b - (pad - CONV_WIDTH // 2)
                    if 0 <= k < CONV_WIDTH:
                        acc = acc + w_ref[k:k + 1, ls] * wb[SUBLANES * a:SUBLANES * a + rc]
            c_ref[:, ls] = acc + b_ref[:, ls]
        v = c_ref[...]
        mu = jnp.mean(v, axis=-1, keepdims=True)
        vc = v - mu
        var = jnp.mean(vc * vc, axis=-1, keepdims=True)
        y = vc * lax.rsqrt(var + NORM_EPS) * g_ref[...] + bb_ref[...]
        o_ref[pl.ds(r0, rc), :] = _silu(y).astype(BF16)
        return carry

    lax.fori_loop(0, t_len // rc, chunk, 0)


def _conv_call(zc, w, b, g, bb, *, nb, t_len, ctx_len):
    m = zc.shape[0]
    ch = CONV_CH
    return pl.pallas_call(
        functools.partial(_conv_body, t_len=t_len, ctx_len=ctx_len),
        grid=(nb,),
        in_specs=[
            pl.BlockSpec((t_len, 2 * ch), lambda i: (i, 0)),
            _const_spec((CONV_WIDTH, ch)),
            _const_spec((1, ch)),
            _const_spec((1, ch)),
            _const_spec((1, ch)),
        ],
        out_specs=pl.BlockSpec((t_len, ch), lambda i: (i, 0)),
        out_shape=jax.ShapeDtypeStruct((m, ch), BF16),
        scratch_shapes=[
            pltpu.VMEM((t_len + 3 * _CONV_PAD, ch), F32),
            pltpu.VMEM((_CONV_ROWS, ch), F32),
        ],
        compiler_params=_cparams("parallel"),
        name="conv",
    )(zc, w, b, g, bb)


_REC_ROWS = 128
_REC_PAD = SUBLANES


def _rec_body(z_ref, cw_ref, cb_ref, wa_ref, ba_ref, wi_ref, bi_ref, lam_ref, o_ref, xp_ref, hf_ref, win_ref, *, t_len,
              ctx_len):
    wd, ch, pad = REC_WIDTH, _REC_ROWS, _REC_PAD
    nch = t_len // ch
    nc_ctx = ctx_len // ch
    nblk = ch // SUBLANES
    zeros = jnp.zeros((pad, wd), F32)
    xp_ref[0:pad] = zeros
    xp_ref[pad + ctx_len:2 * pad + ctx_len] = zeros
    xp_ref[2 * pad + t_len:3 * pad + t_len] = zeros

    def xp_row(i):
        return pl.multiple_of(i * ch + pad + jnp.where(i >= nc_ctx, pad, 0), SUBLANES)

    def fill(i, carry):
        r0 = pl.multiple_of(i * ch, ch)
        xp_ref[pl.ds(xp_row(i), ch), :] = z_ref[pl.ds(r0, ch), 0:wd].astype(F32)
        return carry

    lax.fori_loop(0, nch, fill, 0)

    row8 = lax.broadcasted_iota(jnp.int32, (1, SUBLANES, 1), 1)

    def conv4(i, d):
        start = xp_row(i) - (pad if d == 0 else 0)
        win_ref[...] = xp_ref[pl.ds(pl.multiple_of(start, SUBLANES), ch + pad), :]
        lead = pad - (REC_CONV - 1) if d == 0 else 0
        acc = jnp.zeros((ch, wd), F32) + cb_ref[d]
        taps = cw_ref[d]
        for k in range(REC_CONV):
            acc = acc + taps[k:k + 1, :] * win_ref[lead + k:lead + k + ch, :]
        return acc

    def gates(xc, d):
        xb = xc.astype(BF16)
        r = _sigmoid(jnp.dot(xb, wa_ref[d], preferred_element_type=F32) + ba_ref[d])
        ig = _sigmoid(jnp.dot(xb, wi_ref[d], preferred_element_type=F32) + bi_ref[d])
        lam = lam_ref[d]
        softplus_neg = jnp.maximum(-lam, 0.0) + jnp.log1p(jnp.exp(-jnp.abs(lam)))
        log_a = -LRU_C * r * softplus_neg
        a = jnp.exp(log_a)
        bx = jnp.sqrt(-jnp.tanh(log_a) * (1.0 + a * a)) * (ig * xc)
        return a, bx

    def scan_chunk(a, b, carry, d):
        a = a.reshape(nblk, SUBLANES, wd)
        b = b.reshape(nblk, SUBLANES, wd)
        for s in (1, 2, 4):
            shift, keep = (s, row8 >= s) if d == 0 else (SUBLANES - s, row8 < SUBLANES - s)
            a_s, b_s = pltpu.roll(a, shift, axis=1), pltpu.roll(b, shift, axis=1)
            b = jnp.where(keep, a * b_s + b, b)
            a = jnp.where(keep, a * a_s, a)
        outs = [None] * nblk
        for j in (range(nblk) if d == 0 else reversed(range(nblk))):
            hj = b[j] + a[j] * carry
            outs[j] = hj
            carry = hj[SUBLANES - 1:SUBLANES] if d == 0 else hj[0:1]
        return jnp.concatenate(outs, axis=0), carry

    def fwd(i, carry):
        a, bx = gates(conv4(i, 0), 0)
        h, carry = scan_chunk(a, bx, carry, 0)
        hf_ref[pl.ds(pl.multiple_of(i * ch, ch), ch), :] = h
        return carry

    lax.fori_loop(0, nch, fwd, jnp.zeros((1, wd), F32))

    def bwd(j, carry):
        i = jnp.where(j < nc_ctx, nc_ctx - 1 - j, nch - 1 - (j - nc_ctx))
        a, bx = gates(conv4(i, 1), 1)
        h, carry = scan_chunk(a, bx, carry, 1)
        r0 = pl.multiple_of(i * ch, ch)
        gate = z_ref[pl.ds(r0, ch), wd:2 * wd].astype(F32)
        hsum = hf_ref[pl.ds(r0, ch), :] + h
        o_ref[pl.ds(r0, ch), :] = (hsum * jax.nn.gelu(gate)).astype(BF16)
        return carry

    lax.fori_loop(0, nch, bwd, jnp.zeros((1, wd), F32))


def _rec_call(zr, cw, cb, wa, ba, wi, bi, lam, *, nb, t_len, ctx_len):
    m = zr.shape[0]
    wd = REC_WIDTH
    return pl.pallas_call(
        functools.partial(_rec_body, t_len=t_len, ctx_len=ctx_len),
        grid=(nb,),
        in_specs=[
            pl.BlockSpec((t_len, 2 * wd), lambda i: (i, 0)),
            _const_spec((2, REC_CONV, wd)),
            _const_spec((2, 1, wd)),
            _const_spec((2, wd, wd)),
            _const_spec((2, 1, wd)),
            _const_spec((2, wd, wd)),
            _const_spec((2, 1, wd)),
            _const_spec((2, 1, wd)),
        ],
        out_specs=pl.BlockSpec((t_len, wd), lambda i: (i, 0)),
        out_shape=jax.ShapeDtypeStruct((m, wd), BF16),
        scratch_shapes=[
            pltpu.VMEM((t_len + 3 * _REC_PAD, wd), F32),
            pltpu.VMEM((t_len, wd), F32),
            pltpu.VMEM((_REC_ROWS + _REC_PAD, wd), F32),
        ],
        compiler_params=_cparams("parallel"),
        name="rglru",
    )(zr, cw, cb, wa, ba, wi, bi, lam)


def _qkv_body(zq_ref, zkv_ref, ta_ref, tb_ref, gqa_ref, gkva_ref, wq_ref, wkv_ref, gq_ref, gk_ref, q_ref, k_ref, v_ref):
    hp = HEAD_PAD
    zq = zq_ref[...].astype(F32)
    qa = zq * lax.rsqrt(jnp.mean(zq * zq, axis=-1, keepdims=True) + NORM_EPS) * gqa_ref[...]
    q = jnp.dot(qa.astype(BF16), wq_ref[...], preferred_element_type=F32)

    zkv = zkv_ref[...].astype(F32)
    lora = lax.broadcasted_iota(jnp.int32, (1, KV_IN_PAD), 1) < KV_LORA
    ms = jnp.sum(jnp.where(lora, zkv * zkv, 0.0), axis=-1, keepdims=True) * (1.0 / KV_LORA)
    lhs = jnp.where(lora, zkv * lax.rsqrt(ms + NORM_EPS) * gkva_ref[...], zkv)
    kv = jnp.dot(lhs.astype(BF16), wkv_ref[...], preferred_element_type=F32)
    nh = MLA_HEADS * hp
    lane_v = lax.broadcasted_iota(jnp.int32, (1, nh), 1) & (hp - 1)
    v_ref[...] = (kv[:, 2 * nh:] + jnp.where(lane_v == V_HEAD, 1.0, 0.0)).astype(BF16)

    ta, tb = ta_ref[...], tb_ref[...]
    scale = math.log2(math.e) / math.sqrt(QK_HEAD)
    q_a, q_b = ta * gq_ref[0:1, :] * scale, tb * gq_ref[1:2, :] * scale
    k_a, k_b = ta * gk_ref[0:1, :], tb * gk_ref[1:2, :]
    for h in range(MLA_HEADS):
        sl = slice(h * hp, (h + 1) * hp)
        sp = slice(nh + h * hp, nh + (h + 1) * hp)
        for src, fa, fb, ref in ((q, q_a, q_b, q_ref), (kv, k_a, k_b, k_ref)):
            t, tp = src[:, sl], src[:, sp]
            rs = lax.rsqrt(jnp.sum(t * t, axis=-1, keepdims=True) * (1.0 / QK_HEAD) + NORM_EPS)
            ref[:, sl] = (rs * (t * fa + tp * fb)).astype(BF16)


def _qkv_call(zq, zkv, ta, tb, gqa, gkva, wq, wkv, gq, gk, *, nb, t_len):
    m = zq.shape[0]
    tm = ROW_TILE
    nt = t_len // tm
    row = lambda b, t: (b * nt + t, 0)
    pos = lambda b, t: (t, 0)
    nq = MLA_HEADS * HEAD_PAD
    nv = nq
    return pl.pallas_call(
        _qkv_body,
        grid=(nb, nt),
        in_specs=[
            pl.BlockSpec((tm, Q_LORA), row),
            pl.BlockSpec((tm, KV_IN_PAD), row),
            pl.BlockSpec((tm, HEAD_PAD), pos),
            pl.BlockSpec((tm, HEAD_PAD), pos),
            _const_spec((1, Q_LORA)),
            _const_spec((1, KV_IN_PAD)),
            _const_spec(wq.shape),
            _const_spec(wkv.shape),
            _const_spec((2, HEAD_PAD)),
            _const_spec((2, HEAD_PAD)),
        ],
        out_specs=[pl.BlockSpec((tm, nq), row), pl.BlockSpec((tm, nq), row), pl.BlockSpec((tm, nv), row)],
        out_shape=[jax.ShapeDtypeStruct((m, nq), BF16), jax.ShapeDtypeStruct((m, nq), BF16),
                   jax.ShapeDtypeStruct((m, nv), BF16)],
        compiler_params=_cparams("parallel", "parallel"),
        name="qkv",
    )(zq, zkv, ta, tb, gqa, gkva, wq, wkv, gq, gk)


def _attn_body(q_ref, k_ref, v_ref, o_ref, *, t_len, ctx_len):
    hp = HEAD_PAD
    low = lax.broadcasted_iota(jnp.int32, (1, hp), 1) < V_HEAD

    def attend(nk, ahead):
        def scores(h):
            sl = slice(h * hp, (h + 1) * hp)
            return lax.dot_general(q_ref[:, sl], k_ref[0:nk, sl], (((1,), (1,)), ((), ())),
                                   preferred_element_type=F32)

        queue = [scores(h) for h in range(min(ahead, MLA_HEADS))]
        outs = []
        for h in range(MLA_HEADS):
            s = queue.pop(0)
            if h + ahead < MLA_HEADS:
                queue.append(scores(h + ahead))
            p = jnp.exp2(s - jnp.max(s, axis=-1, keepdims=True)).astype(BF16)
            pv = jnp.dot(p, v_ref[0:nk, h * hp:(h + 1) * hp], preferred_element_type=F32)
            outs.append(pv / pv[:, V_HEAD:V_HEAD + 1])
            if h % 2 == 1:
                both = jnp.where(low, outs[h - 1], pltpu.roll(outs[h], V_HEAD, axis=1))
                o_ref[:, (h // 2) * hp:(h // 2 + 1) * hp] = both.astype(BF16)

    is_ctx_tile = pl.program_id(1) * ROW_TILE < ctx_len

    @pl.when(is_ctx_tile)
    def _():
        attend(ctx_len, ahead=MLA_HEADS)

    @pl.when(jnp.logical_not(is_ctx_tile))
    def _():
        attend(t_len, ahead=2)


def _attn_call(q, k, v, *, nb, t_len, ctx_len):
    m = q.shape[0]
    tq = ROW_TILE
    nt = t_len // tq
    nq = MLA_HEADS * HEAD_PAD
    nv = MLA_HEADS * V_HEAD
    return pl.pallas_call(
        functools.partial(_attn_body, t_len=t_len, ctx_len=ctx_len),
        grid=(nb, nt),
        in_specs=[
            pl.BlockSpec((tq, nq), lambda b, t: (b * nt + t, 0)),
            pl.BlockSpec((t_len, nq), lambda b, t: (b, 0)),
            pl.BlockSpec((t_len, nq), lambda b, t: (b, 0)),
        ],
        out_specs=pl.BlockSpec((tq, nv), lambda b, t: (b * nt + t, 0)),
        out_shape=jax.ShapeDtypeStruct((m, nv), BF16),
        compiler_params=_cparams("parallel", "arbitrary"),
        name="attn",
    )(q, k, v)


_ROUTER_LOW_LANE = 64


def _merge_body(ca_ref, ao_ref, rr_ref, zg_ref, x_ref, ml_ref, mc_ref, bg_ref, g2_ref, wc_ref, wm_ref, wr_ref,
                wo_ref, *rest, tm, ctx_len, n_exp):
    moe = n_exp > 0
    d = x_ref.shape[-1]
    is_ctx = _is_ctx(pl.program_id(1), tm, ctx_len)
    merged = jnp.zeros((tm, d), F32)
    for j, (src, w) in enumerate(((ca_ref, wc_ref), (ao_ref, wm_ref), (rr_ref, wr_ref))):
        gate = _sigmoid(zg_ref[:, j * d:(j + 1) * d].astype(F32) + bg_ref[:, j * d:(j + 1) * d])
        merged = merged + gate * jnp.dot(src[...], w[...], preferred_element_type=F32)
    y = jnp.dot(merged.astype(BF16), wo_ref[...], preferred_element_type=F32)
    x = x_ref[...] + _mod_row(ml_ref, mc_ref, 2, is_ctx) * y
    h2 = _rms_mod(x, g2_ref[...], _mod_row(ml_ref, mc_ref, 3, is_ctx), _mod_row(ml_ref, mc_ref, 4, is_ctx))
    if not moe:
        xo_ref, h_ref = rest
        xo_ref[...] = x
        h_ref[...] = h2.astype(BF16)
        return
    wrt_ref, xo_ref, h_ref, ei_ref, ew_ref = rest
    xo_ref[...] = x
    for s in range(d // LANES):
        h_ref[pl.ds(s, tm, stride=SUBLANES), :] = h2[:, s * LANES:(s + 1) * LANES]
    h_hi = h2.astype(BF16)
    h_lo = (h2 - h_hi.astype(F32)).astype(BF16)
    part = (jnp.dot(h_hi, wrt_ref[...], preferred_element_type=F32)
            + jnp.dot(h_lo, wrt_ref[...], preferred_element_type=F32))
    logits = part + pltpu.roll(part, LANES - _ROUTER_LOW_LANE, axis=1)
    lane = lax.broadcasted_iota(jnp.int32, (tm, LANES), 1).astype(F32)
    logits = jnp.where(lane < n_exp, logits, -jnp.inf)
    m1 = jnp.max(logits, axis=-1, keepdims=True)
    i1 = jnp.min(jnp.where(logits == m1, lane, float(LANES)), axis=-1, keepdims=True)
    rest_l = jnp.where(lane == i1, -jnp.inf, logits)
    m2 = jnp.max(rest_l, axis=-1, keepdims=True)
    i2 = jnp.min(jnp.where(rest_l == m2, lane, float(LANES)), axis=-1, keepdims=True)
    e2 = jnp.exp(m2 - m1)
    w1 = 1.0 / (1.0 + e2)
    w2 = e2 / (1.0 + e2)
    ei_ref[...] = jnp.where(lane == 0.0, i1, jnp.where(lane == 1.0, i2, 0.0)).astype(jnp.int32)
    ew_ref[...] = jnp.where(lane == 0.0, w1, jnp.where(lane == 1.0, w2, 0.0))


def _merge_call(ca, ao, rr, zg, x, ml, mc, bg, g2, wc, wm, wr, wo, w_router, *, nb, t_len, ctx_len):
    m, d = x.shape
    tm = ROW_TILE
    nt = t_len // tm
    row = lambda b, t: (b * nt + t, 0)
    moe = w_router is not None
    n_exp = w_router.shape[1] if moe else 0
    wrt = None
    if moe:
        assert n_exp <= _ROUTER_LOW_LANE
        w_hi = w_router.astype(BF16)
        w_lo = (w_router - w_hi.astype(F32)).astype(BF16)
        wrt = jnp.zeros((d, LANES), BF16).at[:, :n_exp].set(w_hi).at[:, _ROUTER_LOW_LANE:_ROUTER_LOW_LANE + n_exp].set(w_lo)
    in_specs = [
        pl.BlockSpec((tm, ca.shape[1]), row),
        pl.BlockSpec((tm, ao.shape[1]), row),
        pl.BlockSpec((tm, rr.shape[1]), row),
        pl.BlockSpec((tm, zg.shape[1]), row),
        pl.BlockSpec((tm, d), row),
        pl.BlockSpec((1, 6, d), lambda b, t: (b, 0, 0)),
        _const_spec((1, 6, d)),
        _const_spec(bg.shape),
        _const_spec(g2.shape),
        _const_spec(wc.shape),
        _const_spec(wm.shape),
        _const_spec(wr.shape),
        _const_spec(wo.shape),
    ]
    args = [ca, ao, rr, zg, x, ml, mc, bg, g2, wc, wm, wr, wo]
    out_specs = [pl.BlockSpec((tm, d), row)]
    out_shape = [jax.ShapeDtypeStruct((m, d), F32)]
    if moe:
        in_specs.append(_const_spec(wrt.shape))
        args.append(wrt)
        out_specs += [pl.BlockSpec((tm * SUBLANES, LANES), row), pl.BlockSpec((tm, LANES), row),
                      pl.BlockSpec((tm, LANES), row)]
        out_shape += [jax.ShapeDtypeStruct((m * SUBLANES, LANES), F32), jax.ShapeDtypeStruct((m, LANES), jnp.int32),
                      jax.ShapeDtypeStruct((m, LANES), F32)]
    else:
        out_specs.append(pl.BlockSpec((tm, d), row))
        out_shape.append(jax.ShapeDtypeStruct((m, d), BF16))
    return pl.pallas_call(
        functools.partial(_merge_body, tm=tm, ctx_len=ctx_len, n_exp=n_exp),
        grid=(nb, nt),
        in_specs=in_specs,
        out_specs=out_specs,
        out_shape=out_shape,
        input_output_aliases={4: 0},
        compiler_params=_cparams("parallel", "parallel"),
        name="merge_moe" if moe else "merge",
    )(*args)


def _ffn_body(h_ref, x_ref, ml_ref, mc_ref, wg_ref, wu_ref, wd_ref, o_ref, acc_ref, *, tm, ctx_len):
    acc_ref[...] = jnp.zeros_like(acc_ref)

    def chunk(f, carry):
        h = h_ref[...]
        cols = pl.ds(pl.multiple_of(f * tf, tf), tf)
        g = jnp.dot(h, wg_ref[:, cols], preferred_element_type=F32)
        u = jnp.dot(h, wu_ref[:, cols], preferred_element_type=F32)
        acc_ref[...] += jnp.dot((_silu(g) * u).astype(BF16), wd_ref[f], preferred_element_type=F32)
        return carry

    nf, tf = wd_ref.shape[0], wd_ref.shape[1]
    lax.fori_loop(0, nf, chunk, 0)
    is_ctx = _is_ctx(pl.program_id(1), tm, ctx_len)
    o_ref[...] = x_ref[...] + _mod_row(ml_ref, mc_ref, 5, is_ctx) * acc_ref[...]


def _ffn_call(h, x, ml, mc, wg, wu, wd, *, nb, t_len, ctx_len):
    m, d = x.shape
    tm = FFN_ROWS if t_len % FFN_ROWS == 0 else ROW_TILE
    nt = t_len // tm
    row = lambda b, t: (b * nt + t, 0)
    resident = lambda w: pl.BlockSpec(w.shape, lambda b, t: (0,) * w.ndim, pipeline_mode=pl.Buffered(1))
    return pl.pallas_call(
        functools.partial(_ffn_body, tm=tm, ctx_len=ctx_len),
        grid=(nb, nt),
        in_specs=[
            pl.BlockSpec((tm, d), row),
            pl.BlockSpec((tm, d), row),
            pl.BlockSpec((1, 6, d), lambda b, t: (b, 0, 0)),
            _const_spec((1, 6, d)),
            resident(wg),
            resident(wu),
            resident(wd),
        ],
        out_specs=pl.BlockSpec((tm, d), row),
        out_shape=jax.ShapeDtypeStruct((m, d), F32),
        scratch_shapes=[pltpu.VMEM((tm, d), F32)],
        input_output_aliases={1: 0},
        compiler_params=_cparams("parallel", "parallel"),
        name="ffn",
    )(h, x, ml, mc, wg, wu, wd)


def _gather_tiles(idx_ref, src_hbm, dst_ref, slot, sem, n):
    def issue(r, carry):
        src = pl.multiple_of(idx_ref[0, 0, r] * SUBLANES, SUBLANES)
        pltpu.make_async_copy(src_hbm.at[pl.ds(src, SUBLANES), :],
                              dst_ref.at[slot, pl.ds(pl.multiple_of(r * SUBLANES, SUBLANES), SUBLANES), :],
                              sem.at[slot]).start()
        return carry

    lax.fori_loop(0, n, issue, 0, unroll=8)


def _wait_tiles(src_hbm, dst_ref, slot, sem, n):
    pltpu.make_async_copy(src_hbm.at[pl.ds(0, n * SUBLANES), :], dst_ref.at[slot], sem.at[slot]).wait()


def _moe_body(te_ref, nu_ref, st_ref, stn_ref, h_hbm, wg_ref, wu_ref, wd_ref, y_ref, xt_ref, xb_ref, acc_ref, sem, *,
              tm):
    i = pl.program_id(0)
    nf, tf = wd_ref.shape[1], wd_ref.shape[2]
    d = xb_ref.shape[1]
    n_used = nu_ref[0]
    valid = i < n_used
    slot = i & 1

    @pl.when(i == 0)
    def _():
        _gather_tiles(st_ref, h_hbm, xt_ref, 0, sem, tm)

    @pl.when(i <= n_used)
    def _():
        _wait_tiles(h_hbm, xt_ref, slot, sem, tm)

    @pl.when(valid)
    def _():
        for s in range(d // LANES):
            xb_ref[:, s * LANES:(s + 1) * LANES] = xt_ref[slot, pl.ds(s, tm, stride=SUBLANES), :].astype(BF16)
        acc_ref[...] = jnp.zeros_like(acc_ref)
        per_step = tm // MOE_GATHER_CHUNKS

        def chunk(f, carry, gather):
            for r in range(per_step if gather else 0):
                rr = f * per_step + r
                src = pl.multiple_of(stn_ref[0, 0, rr] * SUBLANES, SUBLANES)
                pltpu.make_async_copy(
                    h_hbm.at[pl.ds(src, SUBLANES), :],
                    xt_ref.at[1 - slot, pl.ds(pl.multiple_of(rr * SUBLANES, SUBLANES), SUBLANES), :],
                    sem.at[1 - slot]).start()
            x = xb_ref[...]
            cols = pl.ds(pl.multiple_of(f * tf, tf), tf)
            g = jnp.dot(x, wg_ref[0, :, cols], preferred_element_type=F32)
            u = jnp.dot(x, wu_ref[0, :, cols], preferred_element_type=F32)
            acc_ref[...] += jnp.dot((_silu(g) * u).astype(BF16), wd_ref[0, f], preferred_element_type=F32)
            return carry

        lax.fori_loop(0, MOE_GATHER_CHUNKS, functools.partial(chunk, gather=True), 0)
        lax.fori_loop(MOE_GATHER_CHUNKS, nf, functools.partial(chunk, gather=False), 0)
        for s in range(d // LANES):
            y_ref[pl.ds(s, tm, stride=SUBLANES), :] = acc_ref[:, s * LANES:(s + 1) * LANES]

    @pl.when(jnp.logical_not(valid))
    def _():
        y_ref[...] = jnp.zeros_like(y_ref)


def _moe_call(tile_expert, n_used, slot_token, h_tiles, wg, wu, wd):
    n_exp, d, dff = wg.shape
    nf = wd.shape[1]
    tm = MOE_TILE
    assert tm % MOE_GATHER_CHUNKS == 0 and MOE_GATHER_CHUNKS <= nf
    n_tiles = slot_token.shape[0]
    expert = lambda w: pl.BlockSpec((1,) + w.shape[1:], lambda i, te, nu: (te[i],) + (0,) * (w.ndim - 1),
                                    pipeline_mode=pl.Buffered(1))
    grid_spec = pltpu.PrefetchScalarGridSpec(
        num_scalar_prefetch=2,
        grid=(n_tiles,),
        in_specs=[
            pl.BlockSpec((1, 1, tm), lambda i, te, nu: (i, 0, 0), memory_space=pltpu.SMEM),
            pl.BlockSpec((1, 1, tm), lambda i, te, nu: (jnp.minimum(i + 1, n_tiles - 1), 0, 0),
                         memory_space=pltpu.SMEM),
            pl.BlockSpec(memory_space=pl.ANY),
            expert(wg),
            expert(wu),
            expert(wd),
        ],
        out_specs=pl.BlockSpec((tm * SUBLANES, LANES), lambda i, te, nu: (i, 0)),
        scratch_shapes=[
            pltpu.VMEM((2, tm * SUBLANES, LANES), F32),
            pltpu.VMEM((tm, d), BF16),
            pltpu.VMEM((tm, d), F32),
            pltpu.SemaphoreType.DMA((2,)),
        ],
    )
    return pl.pallas_call(
        functools.partial(_moe_body, tm=tm),
        grid_spec=grid_spec,
        out_shape=jax.ShapeDtypeStruct((n_tiles * tm * SUBLANES, LANES), F32),
        compiler_params=pltpu.CompilerParams(dimension_semantics=("arbitrary",), vmem_limit_bytes=MOE_VMEM_LIMIT),
        name="moe",
    )(tile_expert, n_used, slot_token, slot_token, h_tiles, wg, wu, wd)


def _combine_body(p1_ref, p2_ref, p1n_ref, p2n_ref, y_hbm, ew_ref, x_ref, ml_ref, mc_ref, o_ref, y1_ref, y2_ref, sem,
                  *, tm, ctx_len, tile_of, nt):
    n = pl.program_id(0)
    d = x_ref.shape[1]
    slot = n & 1

    @pl.when(n == 0)
    def _():
        _gather_tiles(p1_ref, y_hbm, y1_ref, 0, sem.at[0], tm)
        _gather_tiles(p2_ref, y_hbm, y2_ref, 0, sem.at[1], tm)

    @pl.when(n + 1 < pl.num_programs(0))
    def _():
        _gather_tiles(p1n_ref, y_hbm, y1_ref, 1 - slot, sem.at[0], tm)
        _gather_tiles(p2n_ref, y_hbm, y2_ref, 1 - slot, sem.at[1], tm)

    _wait_tiles(y_hbm, y1_ref, slot, sem.at[0], tm)
    _wait_tiles(y_hbm, y2_ref, slot, sem.at[1], tm)
    is_ctx = _is_ctx(tile_of(n) % nt, tm, ctx_len)
    w1, w2 = ew_ref[:, 0:1], ew_ref[:, 1:2]
    ga = _mod_row(ml_ref, mc_ref, 5, is_ctx)
    for s in range(d // LANES):
        ls = slice(s * LANES, (s + 1) * LANES)
        f = (w1 * y1_ref[slot, pl.ds(s, tm, stride=SUBLANES), :]
             + w2 * y2_ref[slot, pl.ds(s, tm, stride=SUBLANES), :])
        o_ref[:, ls] = x_ref[:, ls] + ga[:, ls] * f


def _combine_call(pos1, pos2, y_tiles, ew, x, ml, mc, *, nb, t_len, ctx_len, latent_only):
    m, d = x.shape
    tm = ROW_TILE
    nt = t_len // tm
    nc = ctx_len // tm if latent_only else 0
    per = nt - nc
    n_steps = nb * per

    def tile_of(n):
        return (n // per) * nt + nc + n % per

    cur = lambda n: (n, 0, 0)
    nxt = lambda n: (jnp.minimum(n + 1, n_steps - 1), 0, 0)
    row = lambda n: (tile_of(n), 0)
    return pl.pallas_call(
        functools.partial(_combine_body, tm=tm, ctx_len=ctx_len, tile_of=tile_of, nt=nt),
        grid=(n_steps,),
        in_specs=[
            pl.BlockSpec((1, 1, tm), cur, memory_space=pltpu.SMEM),
            pl.BlockSpec((1, 1, tm), cur, memory_space=pltpu.SMEM),
            pl.BlockSpec((1, 1, tm), nxt, memory_space=pltpu.SMEM),
            pl.BlockSpec((1, 1, tm), nxt, memory_space=pltpu.SMEM),
            pl.BlockSpec(memory_space=pl.ANY),
            pl.BlockSpec((tm, LANES), row),
            pl.BlockSpec((tm, d), row),
            pl.BlockSpec((1, 6, d), lambda n: (n // per, 0, 0)),
            _const_spec((1, 6, d)),
        ],
        out_specs=pl.BlockSpec((tm, d), lambda n: (n, 0)),
        out_shape=jax.ShapeDtypeStruct((n_steps * tm, d), F32),
        scratch_shapes=[
            pltpu.VMEM((2, tm * SUBLANES, LANES), F32),
            pltpu.VMEM((2, tm * SUBLANES, LANES), F32),
            pltpu.SemaphoreType.DMA((2, 2)),
        ],
        input_output_aliases={} if latent_only else {6: 0},
        compiler_params=_cparams("arbitrary"),
        name="moe_combine",
    )(pos1, pos2, pos1, pos2, y_tiles, ew, x, ml, mc)


def _route_tables(eidx, token_rows, n_exp, tile):
    m = eidx.shape[0]
    e = eidx.reshape(-1)
    onehot = (e[:, None] == jnp.arange(n_exp, dtype=jnp.int32)[None, :]).astype(jnp.int32)
    csum = jnp.cumsum(onehot, axis=0)
    rank = jnp.sum(csum * onehot, axis=1) - 1
    tiles_per = (csum[-1] + tile - 1) // tile
    tile_end = jnp.cumsum(tiles_per)
    tile_start = tile_end - tiles_per
    pos = jnp.sum(onehot * tile_start[None, :], axis=1) * tile + rank
    n_tiles = -(-(TOP_K * m) // tile) + n_exp
    tidx = jnp.arange(n_tiles, dtype=jnp.int32)
    n_used = tile_end[-1].astype(jnp.int32)
    texp = jnp.sum((tidx[:, None] >= tile_end[None, :]).astype(jnp.int32), axis=1)
    last = jnp.sum((n_used - 1 >= tile_end).astype(jnp.int32))
    texp = jnp.where(tidx < n_used, texp, last).astype(jnp.int32)
    slot_token = jnp.zeros((n_tiles * tile,), jnp.int32).at[pos].set(jnp.repeat(token_rows, TOP_K),
                                                                   unique_indices=True)
    pos = pos.reshape(m, TOP_K).astype(jnp.int32)
    return texp, n_used.reshape(1), slot_token.reshape(n_tiles, 1, tile), pos[:, 0], pos[:, 1]


def _rope_tables(seq, ctx_len):
    rows = seq // GRID_W
    row = jnp.repeat(jnp.arange(rows, dtype=jnp.int32), GRID_W).astype(F32)
    col = jnp.tile(jnp.arange(GRID_W, dtype=jnp.int32), rows).astype(F32)
    half = QK_ROPE // 2
    freqs = ROPE_BASE ** (-jnp.arange(0, half, 2, dtype=F32) / half)
    ar, ac = row[:, None] * freqs, col[:, None] * freqs
    cos = jnp.concatenate([jnp.cos(ar), jnp.cos(ar), jnp.cos(ac), jnp.cos(ac)], axis=1)
    sin = jnp.concatenate([-jnp.sin(ar), jnp.sin(ar), -jnp.sin(ac), jnp.sin(ac)], axis=1)
    ones = jnp.ones((seq, QK_NOPE), F32)
    ta = jnp.concatenate([ones, cos, jnp.ones((seq, HEAD_PAD - QK_HEAD), F32)], axis=1)
    tb = jnp.concatenate([0 * ones, sin, jnp.zeros((seq, HEAD_PAD - QK_HEAD), F32)], axis=1)
    ta = jnp.concatenate([jnp.ones((ctx_len, HEAD_PAD), F32), ta], axis=0)
    tb = jnp.concatenate([jnp.zeros((ctx_len, HEAD_PAD), F32), tb], axis=0)
    return ta, tb


def _rope_partner():
    q = QK_ROPE // 4
    return jnp.array(list(range(q, 2 * q)) + list(range(0, q)) + list(range(3 * q, 4 * q)) + list(range(2 * q, 3 * q)),
                     dtype=jnp.int32)


def _head_gains(g):
    perm = _rope_partner()
    zeros = jnp.zeros((HEAD_PAD - QK_HEAD,), F32)
    g_a = jnp.concatenate([g, zeros])
    g_b = jnp.concatenate([jnp.zeros((QK_NOPE,), F32), g[QK_NOPE:][perm], zeros])
    return jnp.stack([g_a, g_b])


def _pack_wq(w_q_b):
    perm = _rope_partner()
    w = w_q_b.reshape(Q_LORA, MLA_HEADS, QK_HEAD)
    tail = jnp.zeros((Q_LORA, MLA_HEADS, HEAD_PAD - QK_HEAD), F32)
    main = jnp.concatenate([w, tail], axis=-1)
    partner = jnp.concatenate([jnp.zeros((Q_LORA, MLA_HEADS, QK_NOPE), F32), w[:, :, QK_NOPE:][:, :, perm], tail], axis=-1)
    nh = MLA_HEADS * HEAD_PAD
    return jnp.concatenate([main.reshape(Q_LORA, nh), partner.reshape(Q_LORA, nh)], axis=1).astype(BF16)


def _pack_wkv(w_kv_b):
    perm = _rope_partner()
    w = w_kv_b.reshape(KV_LORA, MLA_HEADS, QK_NOPE + V_HEAD)
    eye = jnp.eye(QK_ROPE, dtype=F32)
    left = jnp.zeros((QK_ROPE, QK_NOPE), F32)
    right = jnp.zeros((QK_ROPE, HEAD_PAD - QK_HEAD), F32)
    nh = MLA_HEADS * HEAD_PAD

    def per_head(rope_rows):
        return jnp.broadcast_to(rope_rows[:, None, :], (QK_ROPE, MLA_HEADS, HEAD_PAD)).reshape(QK_ROPE, nh)

    k_top = jnp.concatenate([w[:, :, :QK_NOPE], jnp.zeros((KV_LORA, MLA_HEADS, HEAD_PAD - QK_NOPE), F32)], axis=-1)
    main = jnp.concatenate([k_top.reshape(KV_LORA, nh), per_head(jnp.concatenate([left, eye, right], axis=1))], axis=0)
    partner = jnp.concatenate([jnp.zeros((KV_LORA, nh), F32),
                               per_head(jnp.concatenate([left, eye[:, perm], right], axis=1))], axis=0)
    v_top = jnp.concatenate([w[:, :, QK_NOPE:], jnp.zeros((KV_LORA, MLA_HEADS, HEAD_PAD - V_HEAD), F32)], axis=-1)
    v_cols = jnp.concatenate([v_top.reshape(KV_LORA, nh), jnp.zeros((QK_ROPE, nh), F32)], axis=0)
    full = jnp.concatenate([main, partner, v_cols], axis=1)
    return jnp.pad(full, ((0, KV_IN_PAD - KV_LORA - QK_ROPE), (0, 0))).astype(BF16)


def _pack_w_in(w_in):
    o1 = 2 * CONV_CH
    o2 = o1 + Q_LORA
    o3 = o2 + KV_LORA + QK_ROPE
    kv = jnp.pad(w_in[:, o2:o3], ((0, 0), (0, KV_IN_PAD - KV_LORA - QK_ROPE)))
    return jnp.concatenate([w_in[:, :o2], kv, w_in[:, o3:]], axis=1).astype(BF16)


def _chunk_cols(w):
    *lead, d, dff = w.shape
    w = w.reshape(*lead, d, dff // FFN_CHUNK, FFN_CHUNK)
    return jnp.swapaxes(w, -3, -2).astype(BF16)


def _chunk_rows(w):
    *lead, dff, d = w.shape
    return w.reshape(*lead, dff // FFN_CHUNK, FFN_CHUNK, d).astype(BF16)


def _block_diag(w):
    nd, nblk, bw, _ = w.shape
    eye = jnp.eye(nblk, dtype=w.dtype)
    return jnp.einsum("dgij,gh->dgihj", w, eye).reshape(nd, nblk * bw, nblk * bw).astype(BF16)


def kernel(x, c, ctx, c_ctx, w_ada, b_ada, g_norm1, g_norm2, w_in, b_gate, conv_w, conv_b, conv_ln_g, conv_ln_b, w_o_conv, g_q_a, w_q_b, g_kv_a, w_kv_b, g_qn, g_kn, w_o_mla, rec_conv_w, rec_conv_b, w_ra, b_ra, w_ri, b_ri, lru_lambda, w_o_rec, w_out, w_ff_gate, w_ff_up, w_ff_down, w_router, w_e_gate, w_e_up, w_e_down):
    nb, seq, d = x.shape
    ctx_len = ctx.shape[1]
    depth = w_ada.shape[0]
    t_len = ctx_len + seq
    m = nb * t_len
    n_exp = w_router.shape[-1]
    assert d == 1024 and ctx_len % ROW_TILE == 0 and seq % ROW_TILE == 0 and seq % GRID_W == 0
    dims =dict(nb=nb, t_len=t_len, ctx_len=ctx_len)

    r_pad = -(-(nb + 1) // SUBLANES) * SUBLANES
    cpad = jnp.zeros((r_pad, d), F32).at[:nb].set(c).at[nb].set(c_ctx)
    mods = _ada_call(cpad, w_ada, b_ada).reshape(depth, r_pad, 6, d)

    ta, tb = _rope_tables(seq, ctx_len)
    xs = jnp.concatenate([ctx, x], axis=1).reshape(m, d)

    for i in range(depth):
        moe = i % 2 == 1
        j = i // 2
        ml, mc = mods[i, :nb], mods[i, nb:nb + 1]
        zc, zq, zkv, zr, zg = _inproj_call(xs, ml, mc, g_norm1[i][None], _pack_w_in(w_in[i]), **dims)
        ca = _conv_call(zc, conv_w[i], conv_b[i][None], conv_ln_g[i][None], conv_ln_b[i][None], **dims)
        rr = _rec_call(zr, rec_conv_w[i], rec_conv_b[i][:, None], _block_diag(w_ra[i]), b_ra[i][:, None],
                       _block_diag(w_ri[i]), b_ri[i][:, None], lru_lambda[i][:, None], **dims)
        gkva = jnp.concatenate([g_kv_a[i], jnp.ones((KV_IN_PAD - KV_LORA,), F32)])[None]
        q, k, v = _qkv_call(zq, zkv, ta, tb, g_q_a[i][None], gkva, _pack_wq(w_q_b[i]), _pack_wkv(w_kv_b[i]),
                            _head_gains(g_qn[i]), _head_gains(g_kn[i]), nb=nb, t_len=t_len)
        ao = _attn_call(q, k, v, **dims)
        wrt = w_router[j] if moe else None
        outs = _merge_call(ca, ao, rr, zg, xs, ml, mc, b_gate[i][None], g_norm2[i][None], w_o_conv[i].astype(BF16),
                           w_o_mla[i].astype(BF16), w_o_rec[i].astype(BF16), w_out[i].astype(BF16), wrt, **dims)
        if not moe:
            xs, h2 = outs
            xs = _ffn_call(h2, xs, ml, mc, w_ff_gate[j].astype(BF16), w_ff_up[j].astype(BF16),
                           _chunk_rows(w_ff_down[j]), **dims)
            continue
        xs, h_tiles, eidx, ew = outs
        latent_only = i == depth - 1
        rows = jnp.arange(m, dtype=jnp.int32).reshape(nb, t_len)
        experts = eidx[:, :TOP_K].reshape(nb, t_len, TOP_K)
        if latent_only:
            rows, experts = rows[:, ctx_len:], experts[:, ctx_len:]
        rows = rows.reshape(-1)
        texp, n_used, slot_token, pos1, pos2 = _route_tables(experts.reshape(-1, TOP_K), rows, n_exp, MOE_TILE)
        y_tiles = _moe_call(texp, n_used, slot_token, h_tiles, w_e_gate[j].astype(BF16), w_e_up[j].astype(BF16),
                            _chunk_rows(w_e_down[j]))
        nrt = rows.shape[0] // ROW_TILE
        xs = _combine_call(pos1.reshape(nrt, 1, ROW_TILE), pos2.reshape(nrt, 1, ROW_TILE), y_tiles, ew, xs, ml, mc,
                           latent_only=latent_only, **dims)
        if latent_only:
            return xs.reshape(nb, seq, d)
    return xs.reshape(nb, t_len, d)[:, ctx_len:]
```

```python
import functools
import math

import jax
import jax.numpy as jnp
from jax import lax
from jax.experimental import pallas as pl
from jax.experimental.pallas import tpu as pltpu

F32 = jnp.float32
BF16 = jnp.bfloat16

NORM_EPS = 1e-6
GRID_W = 64
CONV_CH = 512
CONV_WIDTH = 31
MLA_HEADS = 8
QK_NOPE = 64
QK_ROPE = 32
V_HEAD = 64
Q_LORA = 768
KV_LORA = 256
QK_HEAD = QK_NOPE + QK_ROPE
REC_WIDTH = 512
REC_CONV = 4
LRU_C = 8.0
N_BRANCH = 3
TOP_K = 2
ROPE_BASE = 10000.0

LANES = 128
SUBLANES = 8
HEAD_PAD = LANES
KV_IN_PAD = 384
ROW_TILE = 256
FFN_CHUNK = 512
FFN_ROWS = 768
ADA_COLS = 1536
MOE_TILE = 896
MOE_GATHER_CHUNKS = 4
VMEM_LIMIT = 56 * 1024 * 1024
MOE_VMEM_LIMIT = 60 * 1024 * 1024


def _cparams(*sem):
    return pltpu.CompilerParams(dimension_semantics=sem, vmem_limit_bytes=VMEM_LIMIT)


def _const_spec(shape):
    nd = len(shape)
    return pl.BlockSpec(shape, lambda *_: (0,) * nd)


def _sigmoid(x):
    return 0.5 * jnp.tanh(0.5 * x) + 0.5


def _silu(x):
    hx = 0.5 * x
    return hx * jnp.tanh(hx) + hx


def _mod_row(ml_ref, mc_ref, k, is_ctx):
    return jnp.where(is_ctx, mc_ref[0, k:k + 1, :], ml_ref[0, k:k + 1, :])


def _is_ctx(t, tm, ctx_len):
    row = t * tm + lax.broadcasted_iota(jnp.int32, (tm, 1), 0)
    return row < ctx_len


def _rms_mod(x, g, shift, scale):
    y = x * lax.rsqrt(jnp.mean(x * x, axis=-1, keepdims=True) + NORM_EPS) * g
    return y * (1.0 + scale) + shift


def _ada_body(c_ref, w_ref, b_ref, o_ref):
    c = c_ref[...]
    s = _silu(c).astype(BF16)
    o_ref[0] = jnp.dot(s, w_ref[0].astype(BF16), preferred_element_type=F32) + b_ref[0]


def _ada_call(cpad, w_ada, b_ada):
    depth, d, n = w_ada.shape
    r = cpad.shape[0]
    tn = ADA_COLS
    return pl.pallas_call(
        _ada_body,
        grid=(depth, n // tn),
        in_specs=[
            pl.BlockSpec((r, d), lambda l, j: (0, 0)),
            pl.BlockSpec((1, d, tn), lambda l, j: (l, 0, j)),
            pl.BlockSpec((1, 1, tn), lambda l, j: (l, 0, j)),
        ],
        out_specs=pl.BlockSpec((1, r, tn), lambda l, j: (l, 0, j)),
        out_shape=jax.ShapeDtypeStruct((depth, r, n), F32),
        compiler_params=_cparams("arbitrary", "arbitrary"),
        name="ada",
    )(cpad, w_ada, b_ada.reshape(depth, 1, n))


_SEG_WIDTHS = (2 * CONV_CH, Q_LORA, KV_IN_PAD, 2 * REC_WIDTH, N_BRANCH * 1024)
_DOT_COLS = 512


def _inproj_body(x_ref, ml_ref, mc_ref, g_ref, w_ref, *out_refs, tm, ctx_len):
    is_ctx = _is_ctx(pl.program_id(1), tm, ctx_len)
    h = _rms_mod(x_ref[...], g_ref[...], _mod_row(ml_ref, mc_ref, 0, is_ctx), _mod_row(ml_ref, mc_ref, 1, is_ctx))
    hb = h.astype(BF16)
    c0 = 0
    for ref, width in zip(out_refs, _SEG_WIDTHS):
        for j in range(0, width, _DOT_COLS):
            cw = min(_DOT_COLS, width - j)
            ref[:, j:j + cw] = jnp.dot(hb, w_ref[:, c0 + j:c0 + j + cw], preferred_element_type=F32).astype(BF16)
        c0 += width


def _inproj_call(x, ml, mc, g, w, *, nb, t_len, ctx_len):
    m, d = x.shape
    tm = ROW_TILE
    nt = t_len // tm
    row = lambda b, t: (b * nt + t, 0)
    return pl.pallas_call(
        functools.partial(_inproj_body, tm=tm, ctx_len=ctx_len),
        grid=(nb, nt),
        in_specs=[
            pl.BlockSpec((tm, d), row),
            pl.BlockSpec((1, 6, d), lambda b, t: (b, 0, 0)),
            _const_spec((1, 6, d)),
            _const_spec((1, d)),
            pl.BlockSpec(w.shape, lambda b, t: (0, 0), pipeline_mode=pl.Buffered(1)),
        ],
        out_specs=[pl.BlockSpec((tm, wd), row) for wd in _SEG_WIDTHS],
        out_shape=[jax.ShapeDtypeStruct((m, wd), BF16) for wd in _SEG_WIDTHS],
        compiler_params=_cparams("parallel", "parallel"),
        name="inproj",
    )(x, ml, mc, g, w)


_CONV_PAD = 16
_CONV_ROWS = 128


def _conv_body(z_ref, w_ref, b_ref, g_ref, bb_ref, o_ref, u_ref, c_ref, *, t_len, ctx_len):
    ch, pad, rc = CONV_CH, _CONV_PAD, _CONV_ROWS
    zeros = jnp.zeros((pad, ch), F32)
    u_ref[0:pad] = zeros
    u_ref[pad + ctx_len:2 * pad + ctx_len] = zeros
    u_ref[2 * pad + t_len:3 * pad + t_len] = zeros

    def u_row(r0):
        return pl.multiple_of(r0 + pad + jnp.where(r0 >= ctx_len, pad, 0), SUBLANES)

    def glu(i, carry):
        r0 = pl.multiple_of(i * rc, rc)
        z = z_ref[pl.ds(r0, rc), :].astype(F32)
        u_ref[pl.ds(u_row(r0), rc), :] = z[:, :ch] * _sigmoid(z[:, ch:])
        return carry

    lax.fori_loop(0, t_len // rc, glu, 0)

    win = rc + 2 * pad

    def chunk(i, carry):
        r0 = pl.multiple_of(i * rc, rc)
        base = pl.multiple_of(u_row(r0) - pad, SUBLANES)
        for cb in range(ch // LANES):
            ls = slice(cb * LANES, (cb + 1) * LANES)
            w = u_ref[pl.ds(base, win), ls]
            acc = jnp.zeros((rc, LANES), F32)
            for b in range(SUBLANES):
                wb = w if b == 0 else pltpu.roll(w, win - b, axis=0)
                for a in range(win // SUBLANES):
                    k = SUBLANES * a + b - (pad - CONV_WIDTH // 2)
                    if 0 <= k < CONV_WIDTH:
                        acc = acc + w_ref[k:k + 1, ls] * wb[SUBLANES * a:SUBLANES * a + rc]
            c_ref[:, ls] = acc + b_ref[:, ls]
        v = c_ref[...]
        mu = jnp.mean(v, axis=-1, keepdims=True)
        vc = v - mu
        var = jnp.mean(vc * vc, axis=-1, keepdims=True)
        y = vc * lax.rsqrt(var + NORM_EPS) * g_ref[...] + bb_ref[...]
        o_ref[pl.ds(r0, rc), :] = _silu(y).astype(BF16)
        return carry

    lax.fori_loop(0, t_len // rc, chunk, 0)


def _conv_call(zc, w, b, g, bb, *, nb, t_len, ctx_len):
    m = zc.shape[0]
    ch = CONV_CH
    return pl.pallas_call(
        functools.partial(_conv_body, t_len=t_len, ctx_len=ctx_len),
        grid=(nb,),
        in_specs=[
            pl.BlockSpec((t_len, 2 * ch), lambda i: (i, 0)),
            _const_spec((CONV_WIDTH, ch)),
            _const_spec((1, ch)),
            _const_spec((1, ch)),
            _const_spec((1, ch)),
        ],
        out_specs=pl.BlockSpec((t_len, ch), lambda i: (i, 0)),
        out_shape=jax.ShapeDtypeStruct((m, ch), BF16),
        scratch_shapes=[
            pltpu.VMEM((t_len + 3 * _CONV_PAD, ch), F32),
            pltpu.VMEM((_CONV_ROWS, ch), F32),
        ],
        compiler_params=_cparams("parallel"),
        name="conv",
    )(zc, w, b, g, bb)


_REC_ROWS = 128
_REC_PAD = SUBLANES


def _rec_body(z_ref, cw_ref, cb_ref, wa_ref, ba_ref, wi_ref, bi_ref, lam_ref, o_ref, xp_ref, hf_ref, win_ref, *, t_len,
              ctx_len):
    wd, ch, pad = REC_WIDTH, _REC_ROWS, _REC_PAD
    nch = t_len // ch
    nc_ctx = ctx_len // ch
    nblk = ch // SUBLANES
    zeros = jnp.zeros((pad, wd), F32)
    xp_ref[0:pad] = zeros
    xp_ref[pad + ctx_len:2 * pad + ctx_len] = zeros
    xp_ref[2 * pad + t_len:3 * pad + t_len] = zeros

    def xp_row(i):
        return pl.multiple_of(i * ch + pad + jnp.where(i >= nc_ctx, pad, 0), SUBLANES)

    def fill(i, carry):
        r0 = pl.multiple_of(i * ch, ch)
        xp_ref[pl.ds(xp_row(i), ch), :] = z_ref[pl.ds(r0, ch), 0:wd].astype(F32)
        return carry

    lax.fori_loop(0, nch, fill, 0)

    row8 = lax.broadcasted_iota(jnp.int32, (1, SUBLANES, 1), 1)

    def conv4(i, d):
        start = xp_row(i) - (pad if d == 0 else 0)
        win_ref[...] = xp_ref[pl.ds(pl.multiple_of(start, SUBLANES), ch + pad), :]
        lead = pad - (REC_CONV - 1) if d == 0 else 0
        acc = jnp.zeros((ch, wd), F32) + cb_ref[d]
        taps = cw_ref[d]
        for k in range(REC_CONV):
            acc = acc + taps[k:k + 1, :] * win_ref[lead + k:lead + k + ch, :]
        return acc

    def gates(xc, d):
        xb = xc.astype(BF16)
        r = _sigmoid(jnp.dot(xb, wa_ref[d], preferred_element_type=F32) + ba_ref[d])
        ig = _sigmoid(jnp.dot(xb, wi_ref[d], preferred_element_type=F32) + bi_ref[d])
        lam = lam_ref[d]
        softplus_neg = jnp.maximum(-lam, 0.0) + jnp.log1p(jnp.exp(-jnp.abs(lam)))
        log_a = -LRU_C * r * softplus_neg
        a = jnp.exp(log_a)
        bx = jnp.sqrt(-jnp.tanh(log_a) * (1.0 + a * a)) * (ig * xc)
        return a, bx

    def scan_chunk(a, b, carry, d):
        a = a.reshape(nblk, SUBLANES, wd)
        b = b.reshape(nblk, SUBLANES, wd)
        for s in (1, 2, 4):
            shift, keep = (s, row8 >= s) if d == 0 else (SUBLANES - s, row8 < SUBLANES - s)
            a_s, b_s = pltpu.roll(a, shift, axis=1), pltpu.roll(b, shift, axis=1)
            b = jnp.where(keep, a * b_s + b, b)
            a = jnp.where(keep, a * a_s, a)
        outs = [None] * nblk
        for j in (range(nblk) if d == 0 else reversed(range(nblk))):
            hj = b[j] + a[j] * carry
            outs[j] = hj
            carry = hj[SUBLANES - 1:SUBLANES] if d == 0 else hj[0:1]
        return jnp.concatenate(outs, axis=0), carry

    def fwd(i, carry):
        a, bx = gates(conv4(i, 0), 0)
        h, carry = scan_chunk(a, bx, carry, 0)
        hf_ref[pl.ds(pl.multiple_of(i * ch, ch), ch), :] = h
        return carry

    lax.fori_loop(0, nch, fwd, jnp.zeros((1, wd), F32))

    def bwd(j, carry):
        i = jnp.where(j < nc_ctx, nc_ctx - 1 - j, nch - 1 - (j - nc_ctx))
        a, bx = gates(conv4(i, 1), 1)
        h, carry = scan_chunk(a, bx, carry, 1)
        r0 = pl.multiple_of(i * ch, ch)
        gate = z_ref[pl.ds(r0, ch), wd:2 * wd].astype(F32)
        hsum = hf_ref[pl.ds(r0, ch), :] + h
        o_ref[pl.ds(r0, ch), :] = (hsum * jax.nn.gelu(gate)).astype(BF16)
        return carry

    lax.fori_loop(0, nch, bwd, jnp.zeros((1, wd), F32))


def _rec_call(zr, cw, cb, wa, ba, wi, bi, lam, *, nb, t_len, ctx_len):
    m = zr.shape[0]
    wd = REC_WIDTH
    return pl.pallas_call(
        functools.partial(_rec_body, t_len=t_len, ctx_len=ctx_len),
        grid=(nb,),
        in_specs=[
            pl.BlockSpec((t_len, 2 * wd), lambda i: (i, 0)),
            _const_spec((2, REC_CONV, wd)),
            _const_spec((2, 1, wd)),
            _const_spec((2, wd, wd)),
            _const_spec((2, 1, wd)),
            _const_spec((2, wd, wd)),
            _const_spec((2, 1, wd)),
            _const_spec((2, 1, wd)),
        ],
        out_specs=pl.BlockSpec((t_len, wd), lambda i: (i, 0)),
        out_shape=jax.ShapeDtypeStruct((m, wd), BF16),
        scratch_shapes=[
            pltpu.VMEM((t_len + 3 * _REC_PAD, wd), F32),
            pltpu.VMEM((t_len, wd), F32),
            pltpu.VMEM((_REC_ROWS + _REC_PAD, wd), F32),
        ],
        compiler_params=_cparams("parallel"),
        name="rglru",
    )(zr, cw, cb, wa, ba, wi, bi, lam)


def _qkv_body(zq_ref, zkv_ref, ta_ref, tb_ref, gqa_ref, gkva_ref, wq_ref, wkv_ref, gq_ref, gk_ref, q_ref, k_ref, v_ref):
    hp = HEAD_PAD
    zq = zq_ref[...].astype(F32)
    qa = zq * lax.rsqrt(jnp.mean(zq * zq, axis=-1, keepdims=True) + NORM_EPS) * gqa_ref[...]
    q = jnp.dot(qa.astype(BF16), wq_ref[...], preferred_element_type=F32)

    zkv = zkv_ref[...].astype(F32)
    lora = lax.broadcasted_iota(jnp.int32, (1, KV_IN_PAD), 1) < KV_LORA
    ms = jnp.sum(jnp.where(lora, zkv * zkv, 0.0), axis=-1, keepdims=True) * (1.0 / KV_LORA)
    lhs = jnp.where(lora, zkv * lax.rsqrt(ms + NORM_EPS) * gkva_ref[...], zkv)
    kv = jnp.dot(lhs.astype(BF16), wkv_ref[...], preferred_element_type=F32)
    nh = MLA_HEADS * hp
    lane_v = lax.broadcasted_iota(jnp.int32, (1, nh), 1) & (hp - 1)
    v_ref[...] = (kv[:, 2 * nh:] + jnp.where(lane_v == V_HEAD, 1.0, 0.0)).astype(BF16)

    ta, tb = ta_ref[...], tb_ref[...]
    scale = math.log2(math.e) / math.sqrt(QK_HEAD)
    q_a, q_b = ta * gq_ref[0:1, :] * scale, tb * gq_ref[1:2, :] * scale
    k_a, k_b = ta * gk_ref[0:1, :], tb * gk_ref[1:2, :]
    for h in range(MLA_HEADS):
        sl = slice(h * hp, (h + 1) * hp)
        sp = slice(nh + h * hp, nh + (h + 1) * hp)
        for src, fa, fb, ref in ((q, q_a, q_b, q_ref), (kv, k_a, k_b, k_ref)):
            t, tp = src[:, sl], src[:, sp]
            rs = lax.rsqrt(jnp.sum(t * t, axis=-1, keepdims=True) * (1.0 / QK_HEAD) + NORM_EPS)
            ref[:, sl] = (rs * (t * fa + tp * fb)).astype(BF16)


def _qkv_call(zq, zkv, ta, tb, gqa, gkva, wq, wkv, gq, gk, *, nb, t_len):
    m = zq.shape[0]
    tm = ROW_TILE
    nt = t_len // tm
    row = lambda b, t: (b * nt + t, 0)
    pos = lambda b, t: (t, 0)
    nq = MLA_HEADS * HEAD_PAD
    nv = nq
    return pl.pallas_call(
        _qkv_body,
        grid=(nb, nt),
        in_specs=[
            pl.BlockSpec((tm, Q_LORA), row),
            pl.BlockSpec((tm, KV_IN_PAD), row),
            pl.BlockSpec((tm, HEAD_PAD), pos),
            pl.BlockSpec((tm, HEAD_PAD), pos),
            _const_spec((1, Q_LORA)),
            _const_spec((1, KV_IN_PAD)),
            _const_spec(wq.shape),
            _const_spec(wkv.shape),
            _const_spec((2, HEAD_PAD)),
            _const_spec((2, HEAD_PAD)),
        ],
        out_specs=[pl.BlockSpec((tm, nq), row), pl.BlockSpec((tm, nq), row), pl.BlockSpec((tm, nv), row)],
        out_shape=[jax.ShapeDtypeStruct((m, nq), BF16), jax.ShapeDtypeStruct((m, nq), BF16),
                   jax.ShapeDtypeStruct((m, nv), BF16)],
        compiler_params=_cparams("parallel", "parallel"),
        name="qkv",
    )(zq, zkv, ta, tb, gqa, gkva, wq, wkv, gq, gk)


def _attn_body(q_ref, k_ref, v_ref, o_ref, *, t_len, ctx_len):
    hp = HEAD_PAD
    low = lax.broadcasted_iota(jnp.int32, (1, hp), 1) < V_HEAD

    def attend(nk, ahead):
        def scores(h):
            sl = slice(h * hp, (h + 1) * hp)
            return lax.dot_general(q_ref[:, sl], k_ref[0:nk, sl], (((1,), (1,)), ((), ())),
                                   preferred_element_type=F32)

        queue = [scores(h) for h in range(min(ahead, MLA_HEADS))]
        outs = []
        for h in range(MLA_HEADS):
            s = queue.pop(0)
            if h + ahead < MLA_HEADS:
                queue.append(scores(h + ahead))
            p = jnp.exp2(s - jnp.max(s, axis=-1, keepdims=True)).astype(BF16)
            pv = jnp.dot(p, v_ref[0:nk, h * hp:(h + 1) * hp], preferred_element_type=F32)
            outs.append(pv / pv[:, V_HEAD:V_HEAD + 1])
            if h % 2 == 1:
                both = jnp.where(low, outs[h - 1], pltpu.roll(outs[h], V_HEAD, axis=1))
                o_ref[:, (h // 2) * hp:(h // 2 + 1) * hp] = both.astype(BF16)

    is_ctx_tile = pl.program_id(1) * ROW_TILE < ctx_len

    @pl.when(is_ctx_tile)
    def _():
        attend(ctx_len, ahead=MLA_HEADS)

    @pl.when(jnp.logical_not(is_ctx_tile))
    def _():
        attend(t_len, ahead=2)


def _attn_call(q, k, v, *, nb, t_len, ctx_len):
    m = q.shape[0]
    tq = ROW_TILE
    nt = t_len // tq
    nq = MLA_HEADS * HEAD_PAD
    nv = MLA_HEADS * V_HEAD
    return pl.pallas_call(
        functools.partial(_attn_body, t_len=t_len, ctx_len=ctx_len),
        grid=(nb, nt),
        in_specs=[
            pl.BlockSpec((tq, nq), lambda b, t: (b * nt + t, 0)),
            pl.BlockSpec((t_len, nq), lambda b, t: (b, 0)),
            pl.BlockSpec((t_len, nq), lambda b, t: (b, 0)),
        ],
        out_specs=pl.BlockSpec((tq, nv), lambda b, t: (b * nt + t, 0)),
        out_shape=jax.ShapeDtypeStruct((m, nv), BF16),
        compiler_params=_cparams("parallel", "arbitrary"),
        name="attn",
    )(q, k, v)


_ROUTER_LOW_LANE = 64


def _merge_body(ca_ref, ao_ref, rr_ref, zg_ref, x_ref, ml_ref, mc_ref, bg_ref, g2_ref, wc_ref, wm_ref, wr_ref,
                wo_ref, *rest, tm, ctx_len, n_exp):
    moe = n_exp > 0
    d = x_ref.shape[-1]
    is_ctx = _is_ctx(pl.program_id(1), tm, ctx_len)
    merged = jnp.zeros((tm, d), F32)
    for j, (src, w) in enumerate(((ca_ref, wc_ref), (ao_ref, wm_ref), (rr_ref, wr_ref))):
        gate = _sigmoid(zg_ref[:, j * d:(j + 1) * d].astype(F32) + bg_ref[:, j * d:(j + 1) * d])
        merged = merged + gate * jnp.dot(src[...], w[...], preferred_element_type=F32)
    y = jnp.dot(merged.astype(BF16), wo_ref[...], preferred_element_type=F32)
    x = x_ref[...] + _mod_row(ml_ref, mc_ref, 2, is_ctx) * y
    h2 = _rms_mod(x, g2_ref[...], _mod_row(ml_ref, mc_ref, 3, is_ctx), _mod_row(ml_ref, mc_ref, 4, is_ctx))
    if not moe:
        xo_ref, h_ref = rest
        xo_ref[...] = x
        h_ref[...] = h2.astype(BF16)
        return
    wrt_ref, xo_ref, h_ref, ei_ref, ew_ref = rest
    xo_ref[...] = x
    for s in range(d // LANES):
        h_ref[pl.ds(s, tm, stride=SUBLANES), :] = h2[:, s * LANES:(s + 1) * LANES]
    h_hi = h2.astype(BF16)
    h_lo = (h2 - h_hi.astype(F32)).astype(BF16)
    part = (jnp.dot(h_hi, wrt_ref[...], preferred_element_type=F32)
            + jnp.dot(h_lo, wrt_ref[...], preferred_element_type=F32))
    logits = part + pltpu.roll(part, LANES - _ROUTER_LOW_LANE, axis=1)
    lane = lax.broadcasted_iota(jnp.int32, (tm, LANES), 1).astype(F32)
    logits = jnp.where(lane < n_exp, logits, -jnp.inf)
    m1 = jnp.max(logits, axis=-1, keepdims=True)
    i1 = jnp.min(jnp.where(logits == m1, lane, float(LANES)), axis=-1, keepdims=True)
    rest_l = jnp.where(lane == i1, -jnp.inf, logits)
    m2 = jnp.max(rest_l, axis=-1, keepdims=True)
    i2 = jnp.min(jnp.where(rest_l == m2, lane, float(LANES)), axis=-1, keepdims=True)
    e2 = jnp.exp(m2 - m1)
    w1 = 1.0 / (1.0 + e2)
    w2 = e2 / (1.0 + e2)
    ei_ref[...] = jnp.where(lane == 0.0, i1, jnp.where(lane == 1.0, i2, 0.0)).astype(jnp.int32)
    ew_ref[...] = jnp.where(lane == 0.0, w1, jnp.where(lane == 1.0, w2, 0.0))


def _merge_call(ca, ao, rr, zg, x, ml, mc, bg, g2, wc, wm, wr, wo, w_router, *, nb, t_len, ctx_len):
    m, d = x.shape
    tm = ROW_TILE
    nt = t_len // tm
    row = lambda b, t: (b * nt + t, 0)
    moe = w_router is not None
    n_exp = w_router.shape[1] if moe else 0
    wrt = None
    if moe:
        assert n_exp <= _ROUTER_LOW_LANE
        w_hi = w_router.astype(BF16)
        w_lo = (w_router - w_hi.astype(F32)).astype(BF16)
        wrt = jnp.zeros((d, LANES), BF16).at[:, :n_exp].set(w_hi).at[:, _ROUTER_LOW_LANE:_ROUTER_LOW_LANE + n_exp].set(w_lo)
    in_specs = [
        pl.BlockSpec((tm, ca.shape[1]), row),
        pl.BlockSpec((tm, ao.shape[1]), row),
        pl.BlockSpec((tm, rr.shape[1]), row),
        pl.BlockSpec((tm, zg.shape[1]), row),
        pl.BlockSpec((tm, d), row),
        pl.BlockSpec((1, 6, d), lambda b, t: (b, 0, 0)),
        _const_spec((1, 6, d)),
        _const_spec(bg.shape),
        _const_spec(g2.shape),
        _const_spec(wc.shape),
        _const_spec(wm.shape),
        _const_spec(wr.shape),
        _const_spec(wo.shape),
    ]
    args = [ca, ao, rr, zg, x, ml, mc, bg, g2, wc, wm, wr, wo]
    out_specs = [pl.BlockSpec((tm, d), row)]
    out_shape = [jax.ShapeDtypeStruct((m, d), F32)]
    if moe:
        in_specs.append(_const_spec(wrt.shape))
        args.append(wrt)
        out_specs += [pl.BlockSpec((tm * SUBLANES, LANES), row), pl.BlockSpec((tm, LANES), row),
                      pl.BlockSpec((tm, LANES), row)]
        out_shape += [jax.ShapeDtypeStruct((m * SUBLANES, LANES), F32), jax.ShapeDtypeStruct((m, LANES), jnp.int32),
                      jax.ShapeDtypeStruct((m, LANES), F32)]
    else:
        out_specs.append(pl.BlockSpec((tm, d), row))
        out_shape.append(jax.ShapeDtypeStruct((m, d), BF16))
    return pl.pallas_call(
        functools.partial(_merge_body, tm=tm, ctx_len=ctx_len, n_exp=n_exp),
        grid=(nb, nt),
        in_specs=in_specs,
        out_specs=out_specs,
        out_shape=out_shape,
        input_output_aliases={4: 0},
        compiler_params=_cparams("parallel", "parallel"),
        name="merge_moe" if moe else "merge",
    )(*args)


def _ffn_body(h_ref, x_ref, ml_ref, mc_ref, wg_ref, wu_ref, wd_ref, o_ref, acc_ref, *, tm, ctx_len):
    acc_ref[...] = jnp.zeros_like(acc_ref)

    def chunk(f, carry):
        h = h_ref[...]
        cols = pl.ds(pl.multiple_of(f * tf, tf), tf)
        g = jnp.dot(h, wg_ref[:, cols], preferred_element_type=F32)
        u = jnp.dot(h, wu_ref[:, cols], preferred_element_type=F32)
        acc_ref[...] += jnp.dot((_silu(g) * u).astype(BF16), wd_ref[f], preferred_element_type=F32)
        return carry

    nf, tf = wd_ref.shape[0], wd_ref.shape[1]
    lax.fori_loop(0, nf, chunk, 0)
    is_ctx = _is_ctx(pl.program_id(1), tm, ctx_len)
    o_ref[...] = x_ref[...] + _mod_row(ml_ref, mc_ref, 5, is_ctx) * acc_ref[...]


def _ffn_call(h, x, ml, mc, wg, wu, wd, *, nb, t_len, ctx_len):
    m, d = x.shape
    tm = FFN_ROWS if t_len % FFN_ROWS == 0 else ROW_TILE
    nt = t_len // tm
    row = lambda b, t: (b * nt + t, 0)
    resident = lambda w: pl.BlockSpec(w.shape, lambda b, t: (0,) * w.ndim, pipeline_mode=pl.Buffered(1))
    return pl.pallas_call(
        functools.partial(_ffn_body, tm=tm, ctx_len=ctx_len),
        grid=(nb, nt),
        in_specs=[
            pl.BlockSpec((tm, d), row),
            pl.BlockSpec((tm, d), row),
            pl.BlockSpec((1, 6, d), lambda b, t: (b, 0, 0)),
            _const_spec((1, 6, d)),
            resident(wg),
            resident(wu),
            resident(wd),
        ],
        out_specs=pl.BlockSpec((tm, d), row),
        out_shape=jax.ShapeDtypeStruct((m, d), F32),
        scratch_shapes=[pltpu.VMEM((tm, d), F32)],
        input_output_aliases={1: 0},
        compiler_params=_cparams("parallel", "parallel"),
        name="ffn",
    )(h, x, ml, mc, wg, wu, wd)


def _gather_tiles(idx_ref, src_hbm, dst_ref, slot, sem, n):
    def issue(r, carry):
        src = pl.multiple_of(idx_ref[0, 0, r] * SUBLANES, SUBLANES)
        pltpu.make_async_copy(src_hbm.at[pl.ds(src, SUBLANES), :],
                              dst_ref.at[slot, pl.ds(pl.multiple_of(r * SUBLANES, SUBLANES), SUBLANES), :],
                              sem.at[slot]).start()
        return carry

    lax.fori_loop(0, n, issue, 0, unroll=8)


def _wait_tiles(src_hbm, dst_ref, slot, sem, n):
    pltpu.make_async_copy(src_hbm.at[pl.ds(0, n * SUBLANES), :], dst_ref.at[slot], sem.at[slot]).wait()


def _moe_body(te_ref, nu_ref, st_ref, stn_ref, h_hbm, wg_ref, wu_ref, wd_ref, y_ref, xt_ref, xb_ref, acc_ref, sem, *,
              tm):
    i = pl.program_id(0)
    nf, tf = wd_ref.shape[1], wd_ref.shape[2]
    d = xb_ref.shape[1]
    n_used = nu_ref[0]
    valid = i < n_used
    slot = i & 1

    @pl.when(i == 0)
    def _():
        _gather_tiles(st_ref, h_hbm, xt_ref, 0, sem, tm)

    @pl.when(i <= n_used)
    def _():
        _wait_tiles(h_hbm, xt_ref, slot, sem, tm)

    @pl.when(valid)
    def _():
        for s in range(d // LANES):
            xb_ref[:, s * LANES:(s + 1) * LANES] = xt_ref[slot, pl.ds(s, tm, stride=SUBLANES), :].astype(BF16)
        acc_ref[...] = jnp.zeros_like(acc_ref)
        per_step = tm // MOE_GATHER_CHUNKS

        def chunk(f, carry, gather):
            for r in range(per_step if gather else 0):
                rr = f * per_step + r
                src = pl.multiple_of(stn_ref[0, 0, rr] * SUBLANES, SUBLANES)
                pltpu.make_async_copy(
                    h_hbm.at[pl.ds(src, SUBLANES), :],
                    xt_ref.at[1 - slot, pl.ds(pl.multiple_of(rr * SUBLANES, SUBLANES), SUBLANES), :],
                    sem.at[1 - slot]).start()
            x = xb_ref[...]
            cols = pl.ds(pl.multiple_of(f * tf, tf), tf)
            g = jnp.dot(x, wg_ref[0, :, cols], preferred_element_type=F32)
            u = jnp.dot(x, wu_ref[0, :, cols], preferred_element_type=F32)
            acc_ref[...] += jnp.dot((_silu(g) * u).astype(BF16), wd_ref[0, f], preferred_element_type=F32)
            return carry

        lax.fori_loop(0, MOE_GATHER_CHUNKS, functools.partial(chunk, gather=True), 0)
        lax.fori_loop(MOE_GATHER_CHUNKS, nf, functools.partial(chunk, gather=False), 0)
        for s in range(d // LANES):
            y_ref[pl.ds(s, tm, stride=SUBLANES), :] = acc_ref[:, s * LANES:(s + 1) * LANES]

    @pl.when(jnp.logical_not(valid))
    def _():
        y_ref[...] = jnp.zeros_like(y_ref)


def _moe_call(tile_expert, n_used, slot_token, h_tiles, wg, wu, wd):
    n_exp, d, dff = wg.shape
    nf = wd.shape[1]
    tm = MOE_TILE
    assert tm % MOE_GATHER_CHUNKS == 0 and MOE_GATHER_CHUNKS <= nf
    n_tiles = slot_token.shape[0]
    expert = lambda w: pl.BlockSpec((1,) + w.shape[1:], lambda i, te, nu: (te[i],) + (0,) * (w.ndim - 1),
                                    pipeline_mode=pl.Buffered(1))
    grid_spec = pltpu.PrefetchScalarGridSpec(
        num_scalar_prefetch=2,
        grid=(n_tiles,),
        in_specs=[
            pl.BlockSpec((1, 1, tm), lambda i, te, nu: (i, 0, 0), memory_space=pltpu.SMEM),
            pl.BlockSpec((1, 1, tm), lambda i, te, nu: (jnp.minimum(i + 1, n_tiles - 1), 0, 0),
                         memory_space=pltpu.SMEM),
            pl.BlockSpec(memory_space=pl.ANY),
            expert(wg),
            expert(wu),
            expert(wd),
        ],
        out_specs=pl.BlockSpec((tm * SUBLANES, LANES), lambda i, te, nu: (i, 0)),
        scratch_shapes=[
            pltpu.VMEM((2, tm * SUBLANES, LANES), F32),
            pltpu.VMEM((tm, d), BF16),
            pltpu.VMEM((tm, d), F32),
            pltpu.SemaphoreType.DMA((2,)),
        ],
    )
    return pl.pallas_call(
        functools.partial(_moe_body, tm=tm),
        grid_spec=grid_spec,
        out_shape=jax.ShapeDtypeStruct((n_tiles * tm * SUBLANES, LANES), F32),
        compiler_params=pltpu.CompilerParams(dimension_semantics=("arbitrary",), vmem_limit_bytes=MOE_VMEM_LIMIT),
        name="moe",
    )(tile_expert, n_used, slot_token, slot_token, h_tiles, wg, wu, wd)


def _combine_body(p1_ref, p2_ref, p1n_ref, p2n_ref, y_hbm, ew_ref, x_ref, ml_ref, mc_ref, o_ref, y1_ref, y2_ref, sem,
                  *, tm, ctx_len, tile_of, nt):
    n = pl.program_id(0)
    d = x_ref.shape[1]
    slot = n & 1

    @pl.when(n == 0)
    def _():
        _gather_tiles(p1_ref, y_hbm, y1_ref, 0, sem.at[0], tm)
        _gather_tiles(p2_ref, y_hbm, y2_ref, 0, sem.at[1], tm)

    @pl.when(n + 1 < pl.num_programs(0))
    def _():
        _gather_tiles(p1n_ref, y_hbm, y1_ref, 1 - slot, sem.at[0], tm)
        _gather_tiles(p2n_ref, y_hbm, y2_ref, 1 - slot, sem.at[1], tm)

    _wait_tiles(y_hbm, y1_ref, slot, sem.at[0], tm)
    _wait_tiles(y_hbm, y2_ref, slot, sem.at[1], tm)
    is_ctx = _is_ctx(tile_of(n) % nt, tm, ctx_len)
    w1, w2 = ew_ref[:, 0:1], ew_ref[:, 1:2]
    ga = _mod_row(ml_ref, mc_ref, 5, is_ctx)
    for s in range(d // LANES):
        ls = slice(s * LANES, (s + 1) * LANES)
        f = (w1 * y1_ref[slot, pl.ds(s, tm, stride=SUBLANES), :]
             + w2 * y2_ref[slot, pl.ds(s, tm, stride=SUBLANES), :])
        o_ref[:, ls] = x_ref[:, ls] + ga[:, ls] * f


def _combine_call(pos1, pos2, y_tiles, ew, x, ml, mc, *, nb, t_len, ctx_len, latent_only):
    m, d = x.shape
    tm = ROW_TILE
    nt = t_len // tm
    nc = ctx_len // tm if latent_only else 0
    per = nt - nc
    n_steps = nb * per

    def tile_of(n):
        return (n // per) * nt + nc + n % per

    cur = lambda n: (n, 0, 0)
    nxt = lambda n: (jnp.minimum(n + 1, n_steps - 1), 0, 0)
    row = lambda n: (tile_of(n), 0)
    return pl.pallas_call(
        functools.partial(_combine_body, tm=tm, ctx_len=ctx_len, tile_of=tile_of, nt=nt),
        grid=(n_steps,),
        in_specs=[
            pl.BlockSpec((1, 1, tm), cur, memory_space=pltpu.SMEM),
            pl.BlockSpec((1, 1, tm), cur, memory_space=pltpu.SMEM),
            pl.BlockSpec((1, 1, tm), nxt, memory_space=pltpu.SMEM),
            pl.BlockSpec((1, 1, tm), nxt, memory_space=pltpu.SMEM),
            pl.BlockSpec(memory_space=pl.ANY),
            pl.BlockSpec((tm, LANES), row),
            pl.BlockSpec((tm, d), row),
            pl.BlockSpec((1, 6, d), lambda n: (n // per, 0, 0)),
            _const_spec((1, 6, d)),
        ],
        out_specs=pl.BlockSpec((tm, d), lambda n: (n, 0)),
        out_shape=jax.ShapeDtypeStruct((n_steps * tm, d), F32),
        scratch_shapes=[
            pltpu.VMEM((2, tm * SUBLANES, LANES), F32),
            pltpu.VMEM((2, tm * SUBLANES, LANES), F32),
            pltpu.SemaphoreType.DMA((2, 2)),
        ],
        input_output_aliases={} if latent_only else {6: 0},
        compiler_params=_cparams("arbitrary"),
        name="moe_combine",
    )(pos1, pos2, pos1, pos2, y_tiles, ew, x, ml, mc)


def _route_tables(eidx, token_rows, n_exp, tile):
    m = eidx.shape[0]
    e = eidx.reshape(-1)
    onehot = (e[:, None] == jnp.arange(n_exp, dtype=jnp.int32)[None, :]).astype(jnp.int32)
    csum = jnp.cumsum(onehot, axis=0)
    rank = jnp.sum(csum * onehot, axis=1) - 1
    tiles_per = (csum[-1] + tile - 1) // tile
    tile_end = jnp.cumsum(tiles_per)
    tile_start = tile_end - tiles_per
    pos = jnp.sum(onehot * tile_start[None, :], axis=1) * tile + rank
    n_tiles = -(-(TOP_K * m) // tile) + n_exp
    tidx = jnp.arange(n_tiles, dtype=jnp.int32)
    n_used = tile_end[-1].astype(jnp.int32)
    texp = jnp.sum((tidx[:, None] >= tile_end[None, :]).astype(jnp.int32), axis=1)
    last = jnp.sum((n_used - 1 >= tile_end).astype(jnp.int32))
    texp = jnp.where(tidx < n_used, texp, last).astype(jnp.int32)
    slot_token = jnp.zeros((n_tiles * tile,), jnp.int32).at[pos].set(jnp.repeat(token_rows, TOP_K),
                                                                   unique_indices=True)
    pos = pos.reshape(m, TOP_K).astype(jnp.int32)
    return texp, n_used.reshape(1), slot_token.reshape(n_tiles, 1, tile), pos[:, 0], pos[:, 1]


def _rope_tables(seq, ctx_len):
    rows = seq // GRID_W
    row = jnp.repeat(jnp.arange(rows, dtype=jnp.int32), GRID_W).astype(F32)
    col = jnp.tile(jnp.arange(GRID_W, dtype=jnp.int32), rows).astype(F32)
    half = QK_ROPE // 2
    freqs = ROPE_BASE ** (-jnp.arange(0, half, 2, dtype=F32) / half)
    ar, ac = row[:, None] * freqs, col[:, None] * freqs
    cos = jnp.concatenate([jnp.cos(ar), jnp.cos(ar), jnp.cos(ac), jnp.cos(ac)], axis=1)
    sin = jnp.concatenate([-jnp.sin(ar), jnp.sin(ar), -jnp.sin(ac), jnp.sin(ac)], axis=1)
    ones = jnp.ones((seq, QK_NOPE), F32)
    ta = jnp.concatenate([ones, cos, jnp.ones((seq, HEAD_PAD - QK_HEAD), F32)], axis=1)
    tb = jnp.concatenate([0 * ones, sin, jnp.zeros((seq, HEAD_PAD - QK_HEAD), F32)], axis=1)
    ta = jnp.concatenate([jnp.ones((ctx_len, HEAD_PAD), F32), ta], axis=0)
    tb = jnp.concatenate([jnp.zeros((ctx_len, HEAD_PAD), F32), tb], axis=0)
    return ta, tb


def _rope_partner():
    q = QK_ROPE // 4
    return jnp.array(list(range(q, 2 * q)) + list(range(0, q)) + list(range(3 * q, 4 * q)) + list(range(2 * q, 3 * q)),
                     dtype=jnp.int32)


def _head_gains(g):
    perm = _rope_partner()
    zeros = jnp.zeros((HEAD_PAD - QK_HEAD,), F32)
    g_a = jnp.concatenate([g, zeros])
    g_b = jnp.concatenate([jnp.zeros((QK_NOPE,), F32), g[QK_NOPE:][perm], zeros])
    return jnp.stack([g_a, g_b])


def _pack_wq(w_q_b):
    perm = _rope_partner()
    w = w_q_b.reshape(Q_LORA, MLA_HEADS, QK_HEAD)
    tail = jnp.zeros((Q_LORA, MLA_HEADS, HEAD_PAD - QK_HEAD), F32)
    main = jnp.concatenate([w, tail], axis=-1)
    partner = jnp.concatenate([jnp.zeros((Q_LORA, MLA_HEADS, QK_NOPE), F32), w[:, :, QK_NOPE:][:, :, perm], tail], axis=-1)
    nh = MLA_HEADS * HEAD_PAD
    return jnp.concatenate([main.reshape(Q_LORA, nh), partner.reshape(Q_LORA, nh)], axis=1).astype(BF16)


def _pack_wkv(w_kv_b):
    perm = _rope_partner()
    w = w_kv_b.reshape(KV_LORA, MLA_HEADS, QK_NOPE + V_HEAD)
    eye = jnp.eye(QK_ROPE, dtype=F32)
    left = jnp.zeros((QK_ROPE, QK_NOPE), F32)
    right = jnp.zeros((QK_ROPE, HEAD_PAD - QK_HEAD), F32)
    nh = MLA_HEADS * HEAD_PAD

    def per_head(rope_rows):
        return jnp.broadcast_to(rope_rows[:, None, :], (QK_ROPE, MLA_HEADS, HEAD_PAD)).reshape(QK_ROPE, nh)

    k_top = jnp.concatenate([w[:, :, :QK_NOPE], jnp.zeros((KV_LORA, MLA_HEADS, HEAD_PAD - QK_NOPE), F32)], axis=-1)
    main = jnp.concatenate([k_top.reshape(KV_LORA, nh), per_head(jnp.concatenate([left, eye, right], axis=1))], axis=0)
    partner = jnp.concatenate([jnp.zeros((KV_LORA, nh), F32),
                               per_head(jnp.concatenate([left, eye[:, perm], right], axis=1))], axis=0)
    v_top = jnp.concatenate([w[:, :, QK_NOPE:], jnp.zeros((KV_LORA, MLA_HEADS, HEAD_PAD - V_HEAD), F32)], axis=-1)
    v_cols = jnp.concatenate([v_top.reshape(KV_LORA, nh), jnp.zeros((QK_ROPE, nh), F32)], axis=0)
    full = jnp.concatenate([main, partner, v_cols], axis=1)
    return jnp.pad(full, ((0, KV_IN_PAD - KV_LORA - QK_ROPE), (0, 0))).astype(BF16)


def _pack_w_in(w_in):
    o1 = 2 * CONV_CH
    o2 = o1 + Q_LORA
    o3 = o2 + KV_LORA + QK_ROPE
    kv = jnp.pad(w_in[:, o2:o3], ((0, 0), (0, KV_IN_PAD - KV_LORA - QK_ROPE)))
    return jnp.concatenate([w_in[:, :o2], kv, w_in[:, o3:]], axis=1).astype(BF16)


def _chunk_cols(w):
    *lead, d, dff = w.shape
    w = w.reshape(*lead, d, dff // FFN_CHUNK, FFN_CHUNK)
    return jnp.swapaxes(w, -3, -2).astype(BF16)


def _chunk_rows(w):
    *lead, dff, d = w.shape
    return w.reshape(*lead, dff // FFN_CHUNK, FFN_CHUNK, d).astype(BF16)


def _block_diag(w):
    nd, nblk, bw, _ = w.shape
    eye = jnp.eye(nblk, dtype=w.dtype)
    return jnp.einsum("dgij,gh->dgihj", w, eye).reshape(nd, nblk * bw, nblk * bw).astype(BF16)


def kernel(x, c, ctx, c_ctx, w_ada, b_ada, g_norm1, g_norm2, w_in, b_gate, conv_w, conv_b, conv_ln_g, conv_ln_b, w_o_conv, g_q_a, w_q_b, g_kv_a, w_kv_b, g_qn, g_kn, w_o_mla, rec_conv_w, rec_conv_b, w_ra, b_ra, w_ri, b_ri, lru_lambda, w_o_rec, w_out, w_ff_gate, w_ff_up, w_ff_down, w_router, w_e_gate, w_e_up, w_e_down):
    nb, seq, d = x.shape
    ctx_len = ctx.shape[1]
    depth = w_ada.shape[0]
    t_len = ctx_len + seq
    m = nb * t_len
    n_exp = w_router.shape[-1]
    assert d == 1024 and ctx_len % ROW_TILE == 0 and seq % ROW_TILE == 0 and seq % GRID_W == 0
    dims =dict(nb=nb, t_len=t_len, ctx_len=ctx_len)

    r_pad = -(-(nb + 1) // SUBLANES) * SUBLANES
    cpad = jnp.zeros((r_pad, d), F32).at[:nb].set(c).at[nb].set(c_ctx)
    mods = _ada_call(cpad, w_ada, b_ada).reshape(depth, r_pad, 6, d)

    ta, tb = _rope_tables(seq, ctx_len)
    xs = jnp.concatenate([ctx, x], axis=1).reshape(m, d)

    for i in range(depth):
        moe = i % 2 == 1
        j = i // 2
        ml, mc = mods[i, :nb], mods[i, nb:nb + 1]
        zc, zq, zkv, zr, zg = _inproj_call(xs, ml, mc, g_norm1[i][None], _pack_w_in(w_in[i]), **dims)
        ca = _conv_call(zc, conv_w[i], conv_b[i][None], conv_ln_g[i][None], conv_ln_b[i][None], **dims)
        rr = _rec_call(zr, rec_conv_w[i], rec_conv_b[i][:, None], _block_diag(w_ra[i]), b_ra[i][:, None],
                       _block_diag(w_ri[i]), b_ri[i][:, None], lru_lambda[i][:, None], **dims)
        gkva = jnp.concatenate([g_kv_a[i], jnp.ones((KV_IN_PAD - KV_LORA,), F32)])[None]
        q, k, v = _qkv_call(zq, zkv, ta, tb, g_q_a[i][None], gkva, _pack_wq(w_q_b[i]), _pack_wkv(w_kv_b[i]),
                            _head_gains(g_qn[i]), _head_gains(g_kn[i]), nb=nb, t_len=t_len)
        ao = _attn_call(q, k, v, **dims)
        wrt = w_router[j] if moe else None
        outs = _merge_call(ca, ao, rr, zg, xs, ml, mc, b_gate[i][None], g_norm2[i][None], w_o_conv[i].astype(BF16),
                           w_o_mla[i].astype(BF16), w_o_rec[i].astype(BF16), w_out[i].astype(BF16), wrt, **dims)
        if not moe:
            xs, h2 = outs
            xs = _ffn_call(h2, xs, ml, mc, w_ff_gate[j].astype(BF16), w_ff_up[j].astype(BF16),
                           _chunk_rows(w_ff_down[j]), **dims)
            continue
        xs, h_tiles, eidx, ew = outs
        latent_only = i == depth - 1
        rows = jnp.arange(m, dtype=jnp.int32).reshape(nb, t_len)
        experts = eidx[:, :TOP_K].reshape(nb, t_len, TOP_K)
        if latent_only:
            rows, experts = rows[:, ctx_len:], experts[:, ctx_len:]
        rows = rows.reshape(-1)
        texp, n_used, slot_token, pos1, pos2 = _route_tables(experts.reshape(-1, TOP_K), rows, n_exp, MOE_TILE)
        y_tiles = _moe_call(texp, n_used, slot_token, h_tiles, w_e_gate[j].astype(BF16), w_e_up[j].astype(BF16),
                            _chunk_rows(w_e_down[j]))
        nrt = rows.shape[0] // ROW_TILE
        xs = _combine_call(pos1.reshape(nrt, 1, ROW_TILE), pos2.reshape(nrt, 1, ROW_TILE), y_tiles, ew, xs, ml, mc,
                           latent_only=latent_only, **dims)
        if latent_only:
            return xs.reshape(nb, seq, d)
    return xs.reshape(nb, t_len, d)[:, ctx_len:]
```
